```python
import math
import jax, jax.numpy as jnp
from jax import lax
import numpy as np

D_MODEL = 1024
BATCH = 8
SEQ = 4096
DEPTH = 1
DEC_BATCH = 128
DEC_SEQ = 4
PAST_LEN = 8192
PAGE_SIZE = 128

POOL_WINDOWS = (2, 4, 8, 16)
POOL_GROUPS = len(POOL_WINDOWS)
POOL_WIDTH = D_MODEL // 2
POOL_GROUP_DIM = POOL_WIDTH // POOL_GROUPS
POOL_PAD = max(POOL_WINDOWS) - 1
DIFF_HEADS = 4
DIFF_HEAD_DIM = 64
DIFF_V_DIM = 2 * DIFF_HEAD_DIM
DIFF_QK_WIDTH = DIFF_HEADS * 2 * DIFF_HEAD_DIM
DIFF_WIDTH = DIFF_HEADS * DIFF_V_DIM
N_MEM = 256
CROSS_HEADS = 4
CROSS_HEAD_DIM = 128
CROSS_WIDTH = CROSS_HEADS * CROSS_HEAD_DIM
MIX_WIDTH = POOL_WIDTH + DIFF_WIDTH + CROSS_WIDTH
SPLIT_SIZES = (POOL_WIDTH, POOL_WIDTH, DIFF_QK_WIDTH, DIFF_QK_WIDTH, DIFF_WIDTH, DIFF_WIDTH, CROSS_WIDTH, CROSS_WIDTH)
IN_WIDTH = sum(SPLIT_SIZES)
NUM_BUCKETS = 32
MAX_DISTANCE = 128
Q_BLOCK = 128
EPS = 1e-6
NEG_INF = -1e30

kernel_name = 'hybrid_pool_diffattn_memory_decoder_step'


def _rmsnorm(x, g):
    xf = x.astype(jnp.float32)
    xf = xf * lax.rsqrt(jnp.mean(xf * xf, axis=-1, keepdims=True) + EPS)
    return (xf * g.astype(jnp.float32)).astype(x.dtype)


def _rel_bucket(dist):
    n = jnp.maximum(dist, 0)
    max_exact = NUM_BUCKETS // 2
    nf = jnp.maximum(n, 1).astype(jnp.float32)
    large = max_exact + (jnp.log(nf / max_exact) / math.log(MAX_DISTANCE / max_exact)
                         * (NUM_BUCKETS - max_exact)).astype(jnp.int32)
    large = jnp.minimum(large, NUM_BUCKETS - 1)
    return jnp.where(n < max_exact, n, large)


def _in_proj(x, norm_g, w_in, q_g, k_g, cq_g):
    b, l = x.shape[0], x.shape[1]
    h = _rmsnorm(x, norm_g)
    p = h @ w_in
    points = [int(s) for s in np.cumsum(SPLIT_SIZES)[:-1]]
    u, ga, q, k, v, gb, cq, gc = jnp.split(p, points, axis=-1)
    q = _rmsnorm(q.reshape(b, l, DIFF_HEADS, 2, DIFF_HEAD_DIM), q_g)
    k = _rmsnorm(k.reshape(b, l, DIFF_HEADS, 2, DIFF_HEAD_DIM), k_g)
    v = v.reshape(b, l, DIFF_HEADS, DIFF_V_DIM)
    cq = _rmsnorm(cq.reshape(b, l, CROSS_HEADS, CROSS_HEAD_DIM), cq_g)
    return u, ga, q, k, v, gb, cq, gc


def _pool_mix(u_ext, pos, w, scale):
    b, _, c_dim = u_ext.shape
    n_new = pos.shape[0]
    c = jnp.cumsum(u_ext.astype(jnp.float32), axis=1)
    c = jnp.concatenate([jnp.zeros((b, 1, c_dim), jnp.float32), c], axis=1)
    end = c[:, POOL_PAD + 1:]
    u_new = u_ext[:, POOL_PAD:].astype(jnp.float32)
    outs = []
    for g, win in enumerate(POOL_WINDOWS):
        sl = slice(g * POOL_GROUP_DIM, (g + 1) * POOL_GROUP_DIM)
        start = c[:, POOL_PAD + 1 - win: POOL_PAD + 1 - win + n_new, sl]
        count = jnp.minimum(pos + 1, win).astype(jnp.float32)[None, :, None]
        outs.append((end[..., sl] - start) / count - u_new[..., sl])
    d = jnp.stack(outs, axis=2)
    y = jnp.einsum('blgc,gce->blge', d, w.astype(jnp.float32)) * scale.astype(jnp.float32)
    return y.reshape(b, n_new, POOL_WIDTH).astype(u_ext.dtype)


def _diff_lambda(lq, lk, lam_init):
    lq = lq.astype(jnp.float32)
    lk = lk.astype(jnp.float32)
    return jnp.exp(jnp.sum(lq[0] * lk[0])) - jnp.exp(jnp.sum(lq[1] * lk[1])) + lam_init


def _diff_attend(q, qpos, segments, rel_bias, lam):
    scale = DIFF_HEAD_DIM ** -0.5
    logits = []
    for k, v, kpos in segments:
        dist = qpos[:, None] - kpos[None, :]
        bias = jnp.transpose(rel_bias[_rel_bucket(dist)], (2, 0, 1)).astype(jnp.float32)
        s = jnp.einsum('bqhmd,bkhmd->bhmqk', q, k).astype(jnp.float32) * scale + bias[None, :, None]
        logits.append(jnp.where((dist >= 0)[None, None, None], s, NEG_INF))
    p = jax.nn.softmax(jnp.concatenate(logits, axis=-1), axis=-1)
    a = p[:, :, 0] - lam * p[:, :, 1]
    parts = []
    start = 0
    for k, v, kpos in segments:
        n = kpos.shape[0]
        parts.append(jnp.einsum('bhqk,bkhe->bqhe', a[..., start:start + n].astype(v.dtype), v))
        start = start + n
    out = parts[0]
    for extra in parts[1:]:
        out = out + extra
    return out


def _mem_kv(mem, mem_g, w_kv, ck_g):
    b, m = mem.shape[0], mem.shape[1]
    kv = _rmsnorm(mem, mem_g) @ w_kv
    k, v = jnp.split(kv, 2, axis=-1)
    k = _rmsnorm(k.reshape(b, m, CROSS_HEADS, CROSS_HEAD_DIM), ck_g)
    v = v.reshape(b, m, CROSS_HEADS, CROSS_HEAD_DIM)
    return k, v


def _cross_attend(cq, mk, mv):
    s = jnp.einsum('blhd,bmhd->bhlm', cq, mk).astype(jnp.float32) * CROSS_HEAD_DIM ** -0.5
    p = jax.nn.softmax(s, axis=-1)
    return jnp.einsum('bhlm,bmhd->blhd', p.astype(mv.dtype), mv)


def _out_proj(x, a, ga, o, gb, c, gc, head_g, lam_init, w_out):
    b, l = x.shape[0], x.shape[1]
    od = _rmsnorm(o, head_g) * (1.0 - lam_init)
    mix = jnp.concatenate([
        a * jax.nn.silu(ga),
        od.reshape(b, l, DIFF_WIDTH) * jax.nn.silu(gb),
        c.reshape(b, l, CROSS_WIDTH) * jax.nn.silu(gc)], axis=-1)
    return x + mix @ w_out


def setup_inputs(seed: int = 0) -> dict:
    key = jax.random.key(seed)
    ks = jax.random.split(key, 32)
    f32 = jnp.float32
    n_pages = PAST_LEN // PAGE_SIZE
    n_used = DEC_BATCH * n_pages
    n_pool = n_used + n_used // 4
    nrm = jax.random.normal
    page_table = jax.random.permutation(ks[0], n_pool)[:n_used].reshape(DEC_BATCH, n_pages).astype(jnp.int32)
    return {
        'x_prompt': nrm(ks[1], (BATCH, SEQ, D_MODEL), f32),
        'x_sample': nrm(ks[2], (DEC_BATCH, DEC_SEQ, D_MODEL), f32),
        'mem_prompt': nrm(ks[3], (BATCH, N_MEM, D_MODEL), f32),
        'cache_k': nrm(ks[4], (DEPTH, n_pool, PAGE_SIZE, DIFF_HEADS, 2, DIFF_HEAD_DIM), f32),
        'cache_v': nrm(ks[5], (DEPTH, n_pool, PAGE_SIZE, DIFF_HEADS, DIFF_V_DIM), f32),
        'cache_mem_k': nrm(ks[6], (DEPTH, DEC_BATCH, N_MEM, CROSS_HEADS, CROSS_HEAD_DIM), f32),
        'cache_mem_v': nrm(ks[7], (DEPTH, DEC_BATCH, N_MEM, CROSS_HEADS, CROSS_HEAD_DIM), f32),
        'state_pool': nrm(ks[8], (DEPTH, DEC_BATCH, POOL_PAD, POOL_WIDTH), f32),
        'page_table': page_table,
        'norm_g': 1.0 + 0.02 * nrm(ks[9], (DEPTH, D_MODEL), f32),
        'w_in': nrm(ks[10], (DEPTH, D_MODEL, IN_WIDTH), f32) * D_MODEL ** -0.5,
        'q_norm_g': 1.0 + 0.02 * nrm(ks[11], (DEPTH, DIFF_HEAD_DIM), f32),
        'k_norm_g': 1.0 + 0.02 * nrm(ks[12], (DEPTH, DIFF_HEAD_DIM), f32),
        'cq_norm_g': 1.0 + 0.02 * nrm(ks[13], (DEPTH, CROSS_HEAD_DIM), f32),
        'ck_norm_g': 1.0 + 0.02 * nrm(ks[14], (DEPTH, CROSS_HEAD_DIM), f32),
        'mem_norm_g': 1.0 + 0.02 * nrm(ks[15], (DEPTH, D_MODEL), f32),
        'w_mem_kv': nrm(ks[16], (DEPTH, D_MODEL, 2 * CROSS_WIDTH), f32) * D_MODEL ** -0.5,
        'lam_q': 0.1 * nrm(ks[17], (DEPTH, 2, DIFF_HEAD_DIM), f32),
        'lam_k': 0.1 * nrm(ks[18], (DEPTH, 2, DIFF_HEAD_DIM), f32),
        'head_norm_g': 1.0 + 0.02 * nrm(ks[19], (DEPTH, DIFF_HEADS, DIFF_V_DIM), f32),
        'pool_w': nrm(ks[20], (DEPTH, POOL_GROUPS, POOL_GROUP_DIM, POOL_GROUP_DIM), f32) * POOL_GROUP_DIM ** -0.5,
        'pool_scale': 1.0 + 0.1 * nrm(ks[21], (DEPTH, POOL_GROUPS, POOL_GROUP_DIM), f32),
        'rel_bias': 0.5 * nrm(ks[22], (NUM_BUCKETS, DIFF_HEADS), f32),
        'w_out': nrm(ks[23], (DEPTH, MIX_WIDTH, D_MODEL), f32) * MIX_WIDTH ** -0.5,
    }


def reference(x_prompt, x_sample, mem_prompt, cache_k, cache_v, cache_mem_k, cache_mem_v, state_pool, page_table,
              norm_g, w_in, q_norm_g, k_norm_g, cq_norm_g, ck_norm_g, mem_norm_g, w_mem_kv, lam_q, lam_k,
              head_norm_g, pool_w, pool_scale, rel_bias, w_out):
    n_blocks = SEQ // Q_BLOCK
    pos_p = jnp.arange(SEQ, dtype=jnp.int32)
    pos_s = PAST_LEN + jnp.arange(DEC_SEQ, dtype=jnp.int32)
    pos_past = jnp.arange(PAST_LEN, dtype=jnp.int32)
    xp = x_prompt
    xs = x_sample
    kp_l, vp_l, pp_l, mk_l, mv_l, ks_l, vs_l, ps_l = [], [], [], [], [], [], [], []
    for l in range(DEPTH):
        lam_init = 0.8 - 0.6 * math.exp(-0.3 * l)
        lam = _diff_lambda(lam_q[l], lam_k[l], lam_init)

        u, ga, q, k, v, gb, cq, gc = _in_proj(xp, norm_g[l], w_in[l], q_norm_g[l], k_norm_g[l], cq_norm_g[l])
        u_ext = jnp.concatenate([jnp.zeros((BATCH, POOL_PAD, POOL_WIDTH), u.dtype), u], axis=1)
        a = _pool_mix(u_ext, pos_p, pool_w[l], pool_scale[l])
        qb = q.reshape(BATCH, n_blocks, Q_BLOCK, DIFF_HEADS, 2, DIFF_HEAD_DIM).swapaxes(0, 1)
        pb = pos_p.reshape(n_blocks, Q_BLOCK)
        segs_p = ((k, v, pos_p),)
        ob = lax.map(lambda qp: _diff_attend(qp[0], qp[1], segs_p, rel_bias, lam), (qb, pb))
        o = ob.swapaxes(0, 1).reshape(BATCH, SEQ, DIFF_HEADS, DIFF_V_DIM)
        mk, mv = _mem_kv(mem_prompt, mem_norm_g[l], w_mem_kv[l], ck_norm_g[l])
        c = _cross_attend(cq, mk, mv)
        xp = _out_proj(xp, a, ga, o, gb, c, gc, head_norm_g[l], lam_init, w_out[l])
        kp_l.append(k)
        vp_l.append(v)
        pp_l.append(u_ext[:, -POOL_PAD:])
        mk_l.append(mk)
        mv_l.append(mv)

        u, ga, q, k, v, gb, cq, gc = _in_proj(xs, norm_g[l], w_in[l], q_norm_g[l], k_norm_g[l], cq_norm_g[l])
        u_ext = jnp.concatenate([state_pool[l].astype(u.dtype), u], axis=1)
        a = _pool_mix(u_ext, pos_s, pool_w[l], pool_scale[l])
        k_past = cache_k[l, page_table].reshape(DEC_BATCH, PAST_LEN, DIFF_HEADS, 2, DIFF_HEAD_DIM)
        v_past = cache_v[l, page_table].reshape(DEC_BATCH, PAST_LEN, DIFF_HEADS, DIFF_V_DIM)
        o = _diff_attend(q, pos_s, ((k_past, v_past, pos_past), (k, v, pos_s)), rel_bias, lam)
        c = _cross_attend(cq, cache_mem_k[l], cache_mem_v[l])
        xs = _out_proj(xs, a, ga, o, gb, c, gc, head_norm_g[l], lam_init, w_out[l])
        ks_l.append(k)
        vs_l.append(v)
        ps_l.append(u_ext[:, -POOL_PAD:])

    return (xp, xs, jnp.stack(kp_l), jnp.stack(vp_l), jnp.stack(pp_l), jnp.stack(mk_l), jnp.stack(mv_l),
            jnp.stack(ks_l), jnp.stack(vs_l), jnp.stack(ps_l))
```

```python
import functools
import math

import jax
import jax.numpy as jnp
from jax import lax
from jax.experimental import pallas as pl
from jax.experimental.pallas import tpu as pltpu

F32 = jnp.float32
BF16 = jnp.bfloat16

D_MODEL = 1024
SEG = 512
N_SEG = 8
POOL_WINDOWS = (2, 4, 8, 16)
POOL_PAD = 15
HEADS = 4
HEAD_W = 128
MAP_W = 64
N_MEM = 256
PAGE = 128
NUM_BUCKETS = 32
EPS = 1e-6
NEG_INF = -1e30
LAM_INIT = 0.8 - 0.6 * math.exp(-0.3 * 0)
DIFF_SCALE = MAP_W ** -0.5
CROSS_SCALE = HEAD_W ** -0.5

ROW_TILE = 512
ATT_TILE = 256
PAGES_PER_STEP = 16
SAMPLE_ROWS = 8
VMEM_LIMIT = 48 * 1024 * 1024

_NT = (((1,), (1,)), ((), ()))


def _rms(x, g):
    ms = jnp.mean(x * x, axis=-1, keepdims=True)
    return x * lax.rsqrt(ms + EPS) * g


def _silu(x):
    return x * jax.nn.sigmoid(x)


def _norm_block64(blk, g):
    lo = lax.broadcasted_iota(jnp.int32, (1, HEAD_W), 1) < MAP_W
    sq = blk * blk
    s_lo = jnp.sum(jnp.where(lo, sq, 0.0), axis=-1, keepdims=True)
    s_hi = jnp.sum(jnp.where(lo, 0.0, sq), axis=-1, keepdims=True)
    r = jnp.where(lo, lax.rsqrt(s_lo * (1.0 / MAP_W) + EPS), lax.rsqrt(s_hi * (1.0 / MAP_W) + EPS))
    return blk * r * g


def _norm_block128(blk, g):
    ms = jnp.mean(blk * blk, axis=-1, keepdims=True)
    return blk * lax.rsqrt(ms + EPS) * g


def _lam(lq_ref, lk_ref):
    e = jnp.exp(jnp.sum(lq_ref[...] * lk_ref[...], axis=-1, keepdims=True))
    return e[0:1, :] - e[1:2, :] + LAM_INIT


def _shifted_bias(dist, rel_ref, h):
    n = jnp.maximum(dist, 0)
    nf = jnp.maximum(n, 1).astype(F32)
    large = 16 + (jnp.log(nf / 16) / math.log(128 / 16) * 16).astype(jnp.int32)
    large = jnp.minimum(large, NUM_BUCKETS - 1)
    bucket = jnp.where(n < 16, n, large)
    far = rel_ref[(NUM_BUCKETS - 1) * HEADS + h]
    out = jnp.zeros(dist.shape, F32)
    for b in range(NUM_BUCKETS - 1):
        out = jnp.where(bucket == b, rel_ref[b * HEADS + h] - far, out)
    return out


def _project_segments(h, w_ref, gq_ref, gk_ref, gcq_ref, k_ref, v_ref, qb_ref, kb_ref, vb_ref,
                      cqb_ref, sgb_ref, sgc_ref):
    def seg(s):
        return jnp.dot(h, w_ref[:, s * SEG:(s + 1) * SEG], preferred_element_type=F32)

    q = seg(2)
    for c in range(HEADS):
        sl = slice(c * HEAD_W, (c + 1) * HEAD_W)
        qb_ref[:, sl] = (_norm_block64(q[:, sl], gq_ref[:, sl]) * DIFF_SCALE).astype(qb_ref.dtype)
    k = seg(3)
    for c in range(HEADS):
        sl = slice(c * HEAD_W, (c + 1) * HEAD_W)
        kn = _norm_block64(k[:, sl], gk_ref[:, sl])
        k_ref[:, sl] = kn
        if kb_ref is not None:
            kb_ref[:, sl] = kn.astype(BF16)
    v = seg(4)
    v_ref[...] = v
    if vb_ref is not None:
        vb_ref[...] = v.astype(BF16)
    sgb_ref[...] = _silu(seg(5)).astype(sgb_ref.dtype)
    cq = seg(6)
    for c in range(HEADS):
        sl = slice(c * HEAD_W, (c + 1) * HEAD_W)
        cqb_ref[:, sl] = _norm_block128(cq[:, sl], gcq_ref[:, sl]).astype(cqb_ref.dtype)
    sgc_ref[...] = _silu(seg(7)).astype(sgc_ref.dtype)


def _inproj_prompt_kernel(x_ref, ng_ref, w_ref, gq_ref, gk_ref, gcq_ref, pw_ref, ps_ref,
                          k_ref, v_ref, pool_ref, qb_ref, kb_ref, vb_ref, cqb_ref, sgb_ref, sgc_ref,
                          mixa_ref, uext_ref):
    tm = ROW_TILE
    i = pl.program_id(1)

    @pl.when(i == 0)
    def _():
        uext_ref[0:16, :] = jnp.zeros((16, SEG), F32)

    h = _rms(x_ref[...], ng_ref[...]).astype(BF16)
    u = jnp.dot(h, w_ref[:, 0:SEG], preferred_element_type=F32)
    uext_ref[16:16 + tm, :] = u
    sga = _silu(jnp.dot(h, w_ref[:, SEG:2 * SEG], preferred_element_type=F32))
    pos = i * tm + lax.broadcasted_iota(jnp.int32, (tm, 1), 0)
    for g, win in enumerate(POOL_WINDOWS):
        sl = slice(g * HEAD_W, (g + 1) * HEAD_W)
        ug = u[:, sl]
        acc = ug
        for j in range(1, win):
            acc = acc + uext_ref[16 - j:16 - j + tm, sl]
        cnt = jnp.minimum(pos + 1, win).astype(F32)
        d = acc / cnt - ug
        y = jnp.dot(d.astype(BF16), pw_ref[g], preferred_element_type=F32) * ps_ref[:, sl]
        mixa_ref[:, sl] = (y * sga[:, sl]).astype(BF16)

    @pl.when(i == pl.num_programs(1) - 1)
    def _():
        pool_ref[...] = uext_ref[tm + 1:tm + 16, :]

    uext_ref[0:16, :] = uext_ref[tm:tm + 16, :]

    _project_segments(h, w_ref, gq_ref, gk_ref, gcq_ref, k_ref, v_ref, qb_ref, kb_ref, vb_ref,
                      cqb_ref, sgb_ref, sgc_ref)


def _inproj_prompt(x, ng, w_in, gq, gk, gcq, pw, ps):
    b, s, _ = x.shape
    tm = ROW_TILE
    row = lambda width: pl.BlockSpec((None, tm, width), lambda bi, i: (bi, i, 0))
    const = lambda shape: pl.BlockSpec(shape, lambda bi, i: (0,) * len(shape))
    out_shape = (
        jax.ShapeDtypeStruct((b, s, SEG), F32),
        jax.ShapeDtypeStruct((b, s, SEG), F32),
        jax.ShapeDtypeStruct((b, POOL_PAD, SEG), F32),
        jax.ShapeDtypeStruct((b, s, SEG), BF16),
        jax.ShapeDtypeStruct((b, s, SEG), BF16),
        jax.ShapeDtypeStruct((b, s, SEG), BF16),
        jax.ShapeDtypeStruct((b, s, SEG), BF16),
        jax.ShapeDtypeStruct((b, s, SEG), BF16),
        jax.ShapeDtypeStruct((b, s, SEG), BF16),
        jax.ShapeDtypeStruct((b, s, SEG), BF16),
    )
    out_specs = (row(SEG), row(SEG),
                 pl.BlockSpec((None, POOL_PAD, SEG), lambda bi, i: (bi, 0, 0)),
                 row(SEG), row(SEG), row(SEG), row(SEG), row(SEG), row(SEG), row(SEG))
    return pl.pallas_call(
        _inproj_prompt_kernel,
        out_shape=out_shape,
        grid=(b, s // tm),
        in_specs=[row(D_MODEL), const((1, D_MODEL)), const((D_MODEL, N_SEG * SEG)),
                  const((1, SEG)), const((1, SEG)), const((1, SEG)),
                  const((len(POOL_WINDOWS), HEAD_W, HEAD_W)), const((1, SEG))],
        out_specs=out_specs,
        scratch_shapes=[pltpu.VMEM((tm + 16, SEG), F32)],
        compiler_params=pltpu.CompilerParams(dimension_semantics=("arbitrary", "arbitrary"),
                                             vmem_limit_bytes=VMEM_LIMIT),
        name="inproj_prompt",
    )(x, ng, w_in, gq, gk, gcq, pw, ps)


def _memkv_kernel(mem_ref, mg_ref, w_ref, gck_ref, mk_ref, mv_ref, mkb_ref, mvb_ref):
    h = _rms(mem_ref[...], mg_ref[...]).astype(BF16)
    k = jnp.dot(h, w_ref[:, 0:SEG], preferred_element_type=F32)
    for c in range(HEADS):
        sl = slice(c * HEAD_W, (c + 1) * HEAD_W)
        kn = _norm_block128(k[:, sl], gck_ref[:, sl])
        mk_ref[:, sl] = kn
        mkb_ref[:, sl] = kn.astype(BF16)
    v = jnp.dot(h, w_ref[:, SEG:2 * SEG], preferred_element_type=F32)
    mv_ref[...] = v
    mvb_ref[...] = v.astype(BF16)


def _memkv(mem, mg, w_kv, gck):
    b = mem.shape[0]
    blk = lambda width: pl.BlockSpec((None, N_MEM, width), lambda bi: (bi, 0, 0))
    const = lambda shape: pl.BlockSpec(shape, lambda bi: (0,) * len(shape))
    return pl.pallas_call(
        _memkv_kernel,
        out_shape=(jax.ShapeDtypeStruct((b, N_MEM, SEG), F32), jax.ShapeDtypeStruct((b, N_MEM, SEG), F32),
                   jax.ShapeDtypeStruct((b, N_MEM, SEG), BF16), jax.ShapeDtypeStruct((b, N_MEM, SEG), BF16)),
        grid=(b,),
        in_specs=[blk(D_MODEL), const((1, D_MODEL)), const((D_MODEL, 2 * SEG)), const((1, SEG))],
        out_specs=(blk(SEG), blk(SEG), blk(SEG), blk(SEG)),
        compiler_params=pltpu.CompilerParams(dimension_semantics=("arbitrary",),
                                             vmem_limit_bytes=VMEM_LIMIT),
        name="mem_kv",
    )(mem, mg, w_kv, gck)


def _attn_prompt_kernel(rel_ref, x_ref, qb_ref, kb_ref, vb_ref, cqb_ref, sgb_ref, sgc_ref, mixa_ref,
                        mkb_ref, mvb_ref, wout_ref, hg_ref, lq_ref, lk_ref, o_ref,
                        bdiag_ref, bsub_ref, m_ref, l_ref, acc_ref, mix_ref):
    t = ATT_TILE
    qi = pl.program_id(1)

    @pl.when((pl.program_id(0) == 0) & (qi == 0))
    def _():
        d = lax.broadcasted_iota(jnp.int32, (t, t), 0) - lax.broadcasted_iota(jnp.int32, (t, t), 1)
        for h in range(HEADS):
            bdiag_ref[h] = jnp.where(d >= 0, _shifted_bias(d, rel_ref, h), NEG_INF)
            bsub_ref[h] = _shifted_bias(d + t, rel_ref, h)

    lam = _lam(lq_ref, lk_ref)
    lo = lax.broadcasted_iota(jnp.int32, (1, HEAD_W), 1) < MAP_W
    mix_ref[:, 0:SEG] = mixa_ref[...]

    for h in range(HEADS):
        sl = slice(h * HEAD_W, (h + 1) * HEAD_W)
        qh = qb_ref[:, sl]
        zero = jnp.zeros_like(qh)
        qq = jnp.concatenate([jnp.where(lo, qh, zero), jnp.where(lo, zero, qh)], axis=0)

        def scores(j0, bias):
            kh = kb_ref[pl.ds(j0, t), sl]
            s = lax.dot_general(qq, kh, _NT, preferred_element_type=F32)
            if bias is not None:
                s = s + jnp.concatenate([bias, bias], axis=0)
            return s

        def update(j0, s):
            m_old = m_ref[...]
            m_new = jnp.maximum(m_old, jnp.max(s, axis=-1, keepdims=True))
            alpha = jnp.exp(m_old - m_new)
            p = jnp.exp(s - m_new)
            l_ref[...] = alpha * l_ref[...] + jnp.sum(p, axis=-1, keepdims=True)
            acc_ref[...] = alpha * acc_ref[...] + jnp.dot(p.astype(BF16), vb_ref[pl.ds(j0, t), sl],
                                                          preferred_element_type=F32)
            m_ref[...] = m_new

        d0 = pl.multiple_of(qi * t, t)
        s = scores(d0, bdiag_ref[h])
        m0 = jnp.max(s, axis=-1, keepdims=True)
        p = jnp.exp(s - m0)
        m_ref[...] = m0
        l_ref[...] = jnp.sum(p, axis=-1, keepdims=True)
        acc_ref[...] = jnp.dot(p.astype(BF16), vb_ref[pl.ds(d0, t), sl], preferred_element_type=F32)

        @pl.when(qi >= 1)
        def _():
            j0 = pl.multiple_of((qi - 1) * t, t)
            update(j0, scores(j0, bsub_ref[h]))

        def far_tile(j, carry):
            j0 = pl.multiple_of(j * t, t)
            update(j0, scores(j0, None))
            return carry

        lax.fori_loop(0, jnp.maximum(qi - 1, 0), far_tile, 0)

        o = acc_ref[...] / l_ref[...]
        a = o[0:t, :] - lam * o[t:2 * t, :]
        od = _norm_block128(a, hg_ref[:, sl]) * (1.0 - LAM_INIT)
        mix_ref[:, SEG + h * HEAD_W:SEG + (h + 1) * HEAD_W] = (od * sgb_ref[:, sl].astype(F32)).astype(BF16)

        sc = lax.dot_general(cqb_ref[:, sl], mkb_ref[:, sl], _NT, preferred_element_type=F32) * CROSS_SCALE
        pc = jnp.exp(sc - jnp.max(sc, axis=-1, keepdims=True))
        c = jnp.dot(pc.astype(BF16), mvb_ref[:, sl], preferred_element_type=F32)
        c = c / jnp.sum(pc, axis=-1, keepdims=True)
        mix_ref[:, 2 * SEG + h * HEAD_W:2 * SEG + (h + 1) * HEAD_W] = (c * sgc_ref[:, sl].astype(F32)).astype(BF16)

    o_ref[...] = x_ref[...] + jnp.dot(mix_ref[...], wout_ref[...], preferred_element_type=F32)


def _attn_prompt(rel_flat, x, qb, kb, vb, cqb, sgb, sgc, mixa, mkb, mvb, w_out, hg, lq, lk):
    b, s, _ = x.shape
    t = ATT_TILE
    row = lambda width: pl.BlockSpec((None, t, width), lambda bi, i: (bi, i, 0))
    seq = lambda n: pl.BlockSpec((None, n, SEG), lambda bi, i: (bi, 0, 0))
    const = lambda shape: pl.BlockSpec(shape, lambda bi, i: (0,) * len(shape))
    return pl.pallas_call(
        _attn_prompt_kernel,
        out_shape=jax.ShapeDtypeStruct((b, s, D_MODEL), F32),
        grid=(b, s // t),
        in_specs=[pl.BlockSpec(memory_space=pltpu.SMEM),
                  row(D_MODEL), row(SEG), seq(s), seq(s), row(SEG), row(SEG), row(SEG), row(SEG),
                  seq(N_MEM), seq(N_MEM), const((3 * SEG, D_MODEL)), const((1, SEG)),
                  const((2, MAP_W)), const((2, MAP_W))],
        out_specs=row(D_MODEL),
        scratch_shapes=[pltpu.VMEM((HEADS, t, t), F32), pltpu.VMEM((HEADS, t, t), F32),
                        pltpu.VMEM((2 * t, 1), F32), pltpu.VMEM((2 * t, 1), F32),
                        pltpu.VMEM((2 * t, HEAD_W), F32), pltpu.VMEM((t, 3 * SEG), BF16)],
        compiler_params=pltpu.CompilerParams(dimension_semantics=("arbitrary", "arbitrary"),
                                             vmem_limit_bytes=VMEM_LIMIT),
        name="attn_prompt",
    )(rel_flat, x, qb, kb, vb, cqb, sgb, sgc, mixa, mkb, mvb, w_out, hg, lq, lk)


def _inproj_sample_kernel(x_ref, st_ref, ng_ref, w_ref, gq_ref, gk_ref, gcq_ref, pw_ref, ps_ref,
                          k_ref, v_ref, pool_ref, q_ref, cq_ref, sgb_ref, sgc_ref, mixa_ref, d_ref):
    nb = st_ref.shape[1]
    nt = x_ref.shape[0] // nb
    h = _rms(x_ref[...], ng_ref[...]).astype(BF16)
    u = jnp.dot(h, w_ref[:, 0:SEG], preferred_element_type=F32)
    sga = _silu(jnp.dot(h, w_ref[:, SEG:2 * SEG], preferred_element_type=F32))

    def ext(e, sl):
        if e < POOL_PAD:
            return st_ref[e, :, sl]
        return u[(e - POOL_PAD) * nb:(e - POOL_PAD + 1) * nb, sl]

    for g, win in enumerate(POOL_WINDOWS):
        sl = slice(g * HEAD_W, (g + 1) * HEAD_W)
        for tk in range(nt):
            acc = ext(POOL_PAD + tk, sl)
            for j in range(1, win):
                acc = acc + ext(POOL_PAD + tk - j, sl)
            d_ref[tk * nb:(tk + 1) * nb, sl] = acc / float(win) - ext(POOL_PAD + tk, sl)
        y = jnp.dot(d_ref[:, sl].astype(BF16), pw_ref[g], preferred_element_type=F32) * ps_ref[:, sl]
        mixa_ref[:, sl] = y * sga[:, sl]

    keep = POOL_PAD - nt
    pool_ref[0:keep] = st_ref[nt:POOL_PAD]
    for tk in range(nt):
        pool_ref[keep + tk] = u[tk * nb:(tk + 1) * nb, :]

    _project_segments(h, w_ref, gq_ref, gk_ref, gcq_ref, k_ref, v_ref, q_ref, None, None,
                      cq_ref, sgb_ref, sgc_ref)


def _inproj_sample(x_tm, state_tm, ng, w_in, gq, gk, gcq, pw, ps):
    rows = x_tm.shape[0]
    nb = state_tm.shape[1]
    vmem = pl.BlockSpec(memory_space=pltpu.VMEM)
    mat = jax.ShapeDtypeStruct((rows, SEG), F32)
    return pl.pallas_call(
        _inproj_sample_kernel,
        out_shape=(mat, mat, jax.ShapeDtypeStruct((POOL_PAD, nb, SEG), F32), mat, mat, mat, mat, mat),
        in_specs=[vmem] * 9,
        out_specs=(vmem,) * 8,
        scratch_shapes=[pltpu.VMEM((rows, SEG), F32)],
        compiler_params=pltpu.CompilerParams(vmem_limit_bytes=VMEM_LIMIT),
        name="inproj_sample",
    )(x_tm, state_tm, ng, w_in, gq, gk, gcq, pw, ps)


def _attn_sample_kernel(pt_ref, rel_ref, q_ref, kn_ref, vn_ref, cq_ref, sgb_ref, sgc_ref, mk_ref, mv_ref,
                        hg_ref, lq_ref, lk_ref, *rest):
    npg = PAGES_PER_STEP
    k_pages = rest[0:npg]
    v_pages = rest[npg:2 * npg]
    mixd_ref, mixc_ref = rest[2 * npg:2 * npg + 2]
    blast_ref, bnew_ref, kpad_ref, vpad_ref, m_ref, l_ref, acc_ref = rest[2 * npg + 2:]
    r8 = SAMPLE_ROWS
    nq = HEADS * 2 * r8
    c = pl.program_id(1)
    nc = pl.num_programs(1)

    @pl.when((pl.program_id(0) == 0) & (c == 0))
    def _():
        tok = lax.broadcasted_iota(jnp.int32, (2 * r8, PAGE), 0) & (r8 - 1)
        key = lax.broadcasted_iota(jnp.int32, (2 * r8, PAGE), 1)
        for h in range(HEADS):
            blast_ref[h * 2 * r8:(h + 1) * 2 * r8, :] = _shifted_bias(PAGE + tok - key, rel_ref, h)
            ok = (key <= tok) & (key < kn_ref.shape[0])
            bnew_ref[h * 2 * r8:(h + 1) * 2 * r8, :] = jnp.where(ok, _shifted_bias(tok - key, rel_ref, h), NEG_INF)
        kpad_ref[...] = jnp.zeros(kpad_ref.shape, BF16)
        vpad_ref[...] = jnp.zeros(vpad_ref.shape, BF16)

    lane = lax.broadcasted_iota(jnp.int32, (1, SEG), 1)
    q8 = q_ref[...].astype(BF16)
    zero = jnp.zeros_like(q8)
    qbd = jnp.concatenate(
        [jnp.where((lane >= hm * MAP_W) & (lane < (hm + 1) * MAP_W), q8, zero) for hm in range(2 * HEADS)],
        axis=0)

    @pl.when(c == 0)
    def _():
        m_ref[...] = jnp.full(m_ref.shape, NEG_INF, F32)
        l_ref[...] = jnp.zeros(l_ref.shape, F32)
        acc_ref[...] = jnp.zeros(acc_ref.shape, F32)

    def attend(s_parts, v_parts):
        s = jnp.concatenate(s_parts, axis=1) if len(s_parts) > 1 else s_parts[0]
        m_old = m_ref[...]
        m_new = jnp.maximum(m_old, jnp.max(s, axis=-1, keepdims=True))
        alpha = jnp.exp(m_old - m_new)
        p = jnp.exp(s - m_new)
        l_ref[...] = alpha * l_ref[...] + jnp.sum(p, axis=-1, keepdims=True)
        pv = None
        for i, vp in enumerate(v_parts):
            part = jnp.dot(p[:, i * PAGE:(i + 1) * PAGE].astype(BF16), vp, preferred_element_type=F32)
            pv = part if pv is None else pv + part
        acc_ref[...] = alpha * acc_ref[...] + pv
        m_ref[...] = m_new

    last_bias = jnp.where(c == nc - 1, blast_ref[...], 0.0)
    s_parts = []
    for i in range(npg):
        s = lax.dot_general(qbd, k_pages[i][...].astype(BF16), _NT, preferred_element_type=F32)
        if i == npg - 1:
            s = s + last_bias
        s_parts.append(s)
    attend(s_parts, [v_pages[i][...].astype(BF16) for i in range(npg)])

    @pl.when(c == nc - 1)
    def _():
        ntok = kn_ref.shape[0]
        kpad_ref[0:ntok, :] = kn_ref[...].astype(BF16)
        vpad_ref[0:ntok, :] = vn_ref[...].astype(BF16)
        s_new = lax.dot_general(qbd, kpad_ref[...], _NT, preferred_element_type=F32) + bnew_ref[...]
        attend([s_new], [vpad_ref[...]])

        lam = _lam(lq_ref, lk_ref)
        o = acc_ref[...] / l_ref[...]
        cq8 = cq_ref[...].astype(BF16)
        cqbd = jnp.concatenate(
            [jnp.where((lane >= h * HEAD_W) & (lane < (h + 1) * HEAD_W), cq8, jnp.zeros_like(cq8))
             for h in range(HEADS)], axis=0)
        sc = lax.dot_general(cqbd, mk_ref[...].astype(BF16), _NT, preferred_element_type=F32) * CROSS_SCALE
        pc = jnp.exp(sc - jnp.max(sc, axis=-1, keepdims=True))
        oc = jnp.dot(pc.astype(BF16), mv_ref[...].astype(BF16), preferred_element_type=F32)
        oc = oc / jnp.sum(pc, axis=-1, keepdims=True)
        for h in range(HEADS):
            sl = slice(h * HEAD_W, (h + 1) * HEAD_W)
            a = o[h * 2 * r8:h * 2 * r8 + r8, sl] - lam * o[h * 2 * r8 + r8:(h + 1) * 2 * r8, sl]
            od = _norm_block128(a, hg_ref[:, sl]) * (1.0 - LAM_INIT)
            mixd_ref[:, sl] = od * sgb_ref[:, sl]
            mixc_ref[:, sl] = oc[h * r8:(h + 1) * r8, sl] * sgc_ref[:, sl]


def _attn_sample(page_table, rel_flat, q8, kn, vn, cq8, sgb8, sgc8, mem_k, mem_v, cache_k, cache_v, hg, lq, lk):
    nb, n_pages = page_table.shape
    npg = PAGES_PER_STEP
    r8 = SAMPLE_ROWS
    nq = HEADS * 2 * r8
    ntok = kn.shape[1]
    per_seq = lambda rows: pl.BlockSpec((None, rows, SEG), lambda b, c, pt: (b, 0, 0))
    const = lambda shape: pl.BlockSpec(shape, lambda b, c, pt: (0,) * len(shape))

    def page(i):
        return pl.BlockSpec((None, PAGE, SEG), lambda b, c, pt: (pt[b, c * npg + i], 0, 0))

    grid_spec = pltpu.PrefetchScalarGridSpec(
        num_scalar_prefetch=1,
        grid=(nb, n_pages // npg),
        in_specs=[pl.BlockSpec(memory_space=pltpu.SMEM),
                  per_seq(r8), per_seq(ntok), per_seq(ntok), per_seq(r8), per_seq(r8), per_seq(r8),
                  per_seq(N_MEM), per_seq(N_MEM), const((1, SEG)), const((2, MAP_W)), const((2, MAP_W))]
                 + [page(i) for i in range(npg)] + [page(i) for i in range(npg)],
        out_specs=(per_seq(r8), per_seq(r8)),
        scratch_shapes=[pltpu.VMEM((nq, PAGE), F32), pltpu.VMEM((nq, PAGE), F32),
                        pltpu.VMEM((PAGE, SEG), BF16), pltpu.VMEM((PAGE, SEG), BF16),
                        pltpu.VMEM((nq, 1), F32), pltpu.VMEM((nq, 1), F32), pltpu.VMEM((nq, SEG), F32)],
    )
    out = jax.ShapeDtypeStruct((nb, r8, SEG), F32)
    return pl.pallas_call(
        _attn_sample_kernel,
        out_shape=(out, out),
        grid_spec=grid_spec,
        compiler_params=pltpu.CompilerParams(dimension_semantics=("arbitrary", "arbitrary"),
                                             vmem_limit_bytes=VMEM_LIMIT),
        name="attn_sample",
    )(page_table, rel_flat, q8, kn, vn, cq8, sgb8, sgc8, mem_k, mem_v, hg, lq, lk,
      *([cache_k] * npg), *([cache_v] * npg))


def _outproj_kernel(x_ref, a_ref, d_ref, c_ref, w_ref, o_ref):
    y = jnp.dot(a_ref[...].astype(BF16), w_ref[0:SEG, :], preferred_element_type=F32)
    y = y + jnp.dot(d_ref[...].astype(BF16), w_ref[SEG:2 * SEG, :], preferred_element_type=F32)
    y = y + jnp.dot(c_ref[...].astype(BF16), w_ref[2 * SEG:3 * SEG, :], preferred_element_type=F32)
    o_ref[...] = x_ref[...] + y


def _outproj(x, a, d, c, w_out):
    rows = x.shape[0]
    vmem = pl.BlockSpec(memory_space=pltpu.VMEM)
    return pl.pallas_call(
        _outproj_kernel,
        out_shape=jax.ShapeDtypeStruct((rows, D_MODEL), F32),
        in_specs=[vmem] * 5,
        out_specs=vmem,
        compiler_params=pltpu.CompilerParams(vmem_limit_bytes=VMEM_LIMIT),
        name="outproj_sample",
    )(x, a, d, c, w_out)


def _tile_gain(g, reps):
    return jnp.tile(g.reshape(1, -1), (1, reps))


def kernel(x_prompt, x_sample, mem_prompt, cache_k, cache_v, cache_mem_k, cache_mem_v, state_pool, page_table,
           norm_g, w_in, q_norm_g, k_norm_g, cq_norm_g, ck_norm_g, mem_norm_g, w_mem_kv, lam_q, lam_k,
           head_norm_g, pool_w, pool_scale, rel_bias, w_out):
    depth = w_in.shape[0]
    assert depth == 1
    l = 0
    b, s, _ = x_prompt.shape
    nb, nt, _ = x_sample.shape
    r8 = SAMPLE_ROWS

    ng = norm_g[l].reshape(1, D_MODEL)
    w_in_b = w_in[l].astype(BF16)
    gq = _tile_gain(q_norm_g[l], SEG // MAP_W)
    gk = _tile_gain(k_norm_g[l], SEG // MAP_W)
    gcq = _tile_gain(cq_norm_g[l], HEADS)
    gck = _tile_gain(ck_norm_g[l], HEADS)
    pw = pool_w[l].astype(BF16)
    ps = pool_scale[l].reshape(1, SEG)
    hg = head_norm_g[l].reshape(1, SEG)
    w_out_b = w_out[l].astype(BF16)
    rel_flat = rel_bias.reshape(-1)

    k_p, v_p, pool_p, qb, kb, vb, cqb, sgb, sgc, mixa = _inproj_prompt(x_prompt, ng, w_in_b, gq, gk, gcq, pw, ps)
    mk, mv, mkb, mvb = _memkv(mem_prompt, mem_norm_g[l].reshape(1, D_MODEL), w_mem_kv[l].astype(BF16), gck)
    y_p = _attn_prompt(rel_flat, x_prompt, qb, kb, vb, cqb, sgb, sgc, mixa, mkb, mvb, w_out_b, hg,
                       lam_q[l], lam_k[l])

    x_tm = x_sample.transpose(1, 0, 2).reshape(nt * nb, D_MODEL)
    st_tm = state_pool[l].transpose(1, 0, 2)
    k_s, v_s, pool_s, q_s, cq_s, sgb_s, sgc_s, mixa_s = _inproj_sample(x_tm, st_tm, ng, w_in_b, gq, gk, gcq, pw, ps)

    def seq_major(a, rows):
        a = a.reshape(nt, nb, SEG).transpose(1, 0, 2)
        return a if rows == nt else jnp.pad(a, ((0, 0), (0, rows - nt), (0, 0)))

    kn = seq_major(k_s, nt)
    vn = seq_major(v_s, nt)
    n_pool = cache_k.shape[1]
    mixd, mixc = _attn_sample(
        page_table, rel_flat, seq_major(q_s, r8), kn, vn, seq_major(cq_s, r8), seq_major(sgb_s, r8),
        seq_major(sgc_s, r8), cache_mem_k[l].reshape(nb, N_MEM, SEG), cache_mem_v[l].reshape(nb, N_MEM, SEG),
        cache_k[l].reshape(n_pool, PAGE, SEG), cache_v[l].reshape(n_pool, PAGE, SEG), hg, lam_q[l], lam_k[l])
    x8 = jnp.pad(x_sample, ((0, 0), (0, r8 - nt), (0, 0))).reshape(nb * r8, D_MODEL)
    y8 = _outproj(x8, seq_major(mixa_s, r8).reshape(nb * r8, SEG), mixd.reshape(nb * r8, SEG),
                  mixc.reshape(nb * r8, SEG), w_out_b)
    y_s = y8.reshape(nb, r8, D_MODEL)[:, :nt]

    return (y_p, y_s,
            k_p.reshape(1, b, s, HEADS, 2, MAP_W), v_p.reshape(1, b, s, HEADS, HEAD_W),
            pool_p[None], mk.reshape(1, b, N_MEM, HEADS, HEAD_W), mv.reshape(1, b, N_MEM, HEADS, HEAD_W),
            kn.reshape(1, nb, nt, HEADS, 2, MAP_W), vn.reshape(1, nb, nt, HEADS, HEAD_W),
            pool_s.transpose(1, 0, 2)[None])
```

```python
import functools
import math

import jax
import jax.numpy as jnp
from jax import lax
from jax.experimental import pallas as pl
from jax.experimental.pallas import tpu as pltpu

F32 = jnp.float32
BF16 = jnp.bfloat16

D_MODEL = 1024
SEG = 512
N_SEG = 8
POOL_WINDOWS = (2, 4, 8, 16)
POOL_PAD = 15
HEADS = 4
HEAD_W = 128
MAP_W = 64
N_MEM = 256
PAGE = 128
NUM_BUCKETS = 32
EPS = 1e-6
NEG_INF = -1e30
LAM_INIT = 0.8 - 0.6 * math.exp(-0.3 * 0)
DIFF_SCALE = MAP_W ** -0.5
CROSS_SCALE = HEAD_W ** -0.5

ROW_TILE = 512
ATT_TILE = 256
PAGES_PER_STEP = 16
SAMPLE_ROWS = 8
VMEM_LIMIT = 48 * 1024 * 1024

_NT = (((1,), (1,)), ((), ()))


def _rms(x, g):
    ms = jnp.mean(x * x, axis=-1, keepdims=True)
    return x * lax.rsqrt(ms + EPS) * g


def _silu(x):
    return x * jax.nn.sigmoid(x)


def _norm_block64(blk, g):
    lo = lax.broadcasted_iota(jnp.int32, (1, HEAD_W), 1) < MAP_W
    sq = blk * blk
    s_lo = jnp.sum(jnp.where(lo, sq, 0.0), axis=-1, keepdims=True)
    s_hi = jnp.sum(jnp.where(lo, 0.0, sq), axis=-1, keepdims=True)
    r = jnp.where(lo, lax.rsqrt(s_lo * (1.0 / MAP_W) + EPS), lax.rsqrt(s_hi * (1.0 / MAP_W) + EPS))
    return blk * r * g


def _norm_block128(blk, g):
    ms = jnp.mean(blk * blk, axis=-1, keepdims=True)
    return blk * lax.rsqrt(ms + EPS) * g


def _lam(lq_ref, lk_ref):
    e = jnp.exp(jnp.sum(lq_ref[...] * lk_ref[...], axis=-1, keepdims=True))
    return e[0:1, :] - e[1:2, :] + LAM_INIT


def _shifted_bias(dist, rel_ref, h):
    n = jnp.maximum(dist, 0)
    nf = jnp.maximum(n, 1).astype(F32)
    large = 16 + (jnp.log(nf / 16) / math.log(128 / 16) * 16).astype(jnp.int32)
    large = jnp.minimum(large, NUM_BUCKETS - 1)
    bucket = jnp.where(n < 16, n, large)
    far = rel_ref[(NUM_BUCKETS - 1) * HEADS + h]
    out = jnp.zeros(dist.shape, F32)
    for b in range(NUM_BUCKETS - 1):
        out = jnp.where(bucket == b, rel_ref[b * HEADS + h] - far, out)
    return out


def _project_segments(h, w_ref, gq_ref, gk_ref, gcq_ref, k_ref, v_ref, qb_ref, kb_ref, vb_ref,
                      cqb_ref, sgb_ref, sgc_ref):
    def seg(s):
        return jnp.dot(h, w_ref[:, s * SEG:(s + 1) * SEG], preferred_element_type=F32)

    q = seg(2)
    for c in range(HEADS):
        sl = slice(c * HEAD_W, (c + 1) * HEAD_W)
        qb_ref[:, sl] = (_norm_block64(q[:, sl], gq_ref[:, sl]) * DIFF_SCALE).astype(qb_ref.dtype)
    k = seg(3)
    for c in range(HEADS):
        sl = slice(c * HEAD_W, (c + 1) * HEAD_W)
        kn = _norm_block64(k[:, sl], gk_ref[:, sl])
        k_ref[:, sl] = kn
        if kb_ref is not None:
            kb_ref[:, sl] = kn.astype(BF16)
    v = seg(4)
    v_ref[...] = v
    if vb_ref is not None:
        vb_ref[...] = v.astype(BF16)
    sgb_ref[...] = _silu(seg(5)).astype(sgb_ref.dtype)
    cq = seg(6)
    for c in range(HEADS):
        sl = slice(c * HEAD_W, (c + 1) * HEAD_W)
        cqb_ref[:, sl] = _norm_block128(cq[:, sl], gcq_ref[:, sl]).astype(cqb_ref.dtype)
    sgc_ref[...] = _silu(seg(7)).astype(sgc_ref.dtype)


def _inproj_prompt_kernel(x_ref, ng_ref, w_ref, gq_ref, gk_ref, gcq_ref, pw_ref, ps_ref,
                          k_ref, v_ref, pool_ref, qb_ref, kb_ref, vb_ref, cqb_ref, sgb_ref, sgc_ref,
                          mixa_ref, uext_ref):
    tm = ROW_TILE
    i = pl.program_id(1)

    @pl.when(i == 0)
    def _():
        uext_ref[0:16, :] = jnp.zeros((16, SEG), F32)

    h = _rms(x_ref[...], ng_ref[...]).astype(BF16)
    u = jnp.dot(h, w_ref[:, 0:SEG], preferred_element_type=F32)
    uext_ref[16:16 + tm, :] = u
    sga = _silu(jnp.dot(h, w_ref[:, SEG:2 * SEG], preferred_element_type=F32))
    pos = i * tm + lax.broadcasted_iota(jnp.int32, (tm, 1), 0)
    for g, win in enumerate(POOL_WINDOWS):
        sl = slice(g * HEAD_W, (g + 1) * HEAD_W)
        ug = u[:, sl]
        acc = ug
        for j in range(1, win):
            acc = acc + uext_ref[16 - j:16 - j + tm, sl]
        cnt = jnp.minimum(pos + 1, win).astype(F32)
        d = acc / cnt - ug
        y = jnp.dot(d.astype(BF16), pw_ref[g], preferred_element_type=F32) * ps_ref[:, sl]
        mixa_ref[:, sl] = (y * sga[:, sl]).astype(BF16)

    @pl.when(i == pl.num_programs(1) - 1)
    def _():
        pool_ref[...] = uext_ref[tm + 1:tm + 16, :]

    uext_ref[0:16, :] = uext_ref[tm:tm + 16, :]

    _project_segments(h, w_ref, gq_ref, gk_ref, gcq_ref, k_ref, v_ref, qb_ref, kb_ref, vb_ref,
                      cqb_ref, sgb_ref, sgc_ref)


def _inproj_prompt(x, ng, w_in, gq, gk, gcq, pw, ps):
    b, s, _ = x.shape
    tm = ROW_TILE
    row = lambda width: pl.BlockSpec((None, tm, width), lambda bi, i: (bi, i, 0))
    const = lambda shape: pl.BlockSpec(shape, lambda bi, i: (0,) * len(shape))
    out_shape = (
        jax.ShapeDtypeStruct((b, s, SEG), F32),
        jax.ShapeDtypeStruct((b, s, SEG), F32),
        jax.ShapeDtypeStruct((b, POOL_PAD, SEG), F32),
        jax.ShapeDtypeStruct((b, s, SEG), BF16),
        jax.ShapeDtypeStruct((b, s, SEG), BF16),
        jax.ShapeDtypeStruct((b, s, SEG), BF16),
        jax.ShapeDtypeStruct((b, s, SEG), BF16),
        jax.ShapeDtypeStruct((b, s, SEG), BF16),
        jax.ShapeDtypeStruct((b, s, SEG), BF16),
        jax.ShapeDtypeStruct((b, s, SEG), BF16),
    )
    out_specs = (row(SEG), row(SEG),
                 pl.BlockSpec((None, POOL_PAD, SEG), lambda bi, i: (bi, 0, 0)),
                 row(SEG), row(SEG), row(SEG), row(SEG), row(SEG), row(SEG), row(SEG))
    return pl.pallas_call(
        _inproj_prompt_kernel,
        out_shape=out_shape,
        grid=(b, s // tm),
        in_specs=[row(D_MODEL), const((1, D_MODEL)), const((D_MODEL, N_SEG * SEG)),
                  const((1, SEG)), const((1, SEG)), const((1, SEG)),
                  const((len(POOL_WINDOWS), HEAD_W, HEAD_W)), const((1, SEG))],
        out_specs=out_specs,
        scratch_shapes=[pltpu.VMEM((tm + 16, SEG), F32)],
        compiler_params=pltpu.CompilerParams(dimension_semantics=("arbitrary", "arbitrary"),
                                             vmem_limit_bytes=VMEM_LIMIT),
        name="inproj_prompt",
    )(x, ng, w_in, gq, gk, gcq, pw, ps)


def _memkv_kernel(mem_ref, mg_ref, w_ref, gck_ref, mk_ref, mv_ref, mkb_ref, mvb_ref):
    h = _rms(mem_ref[...], mg_ref[...]).astype(BF16)
    k = jnp.dot(h, w_ref[:, 0:SEG], preferred_element_type=F32)
    for c in range(HEADS):
        sl = slice(c * HEAD_W, (c + 1) * HEAD_W)
        kn = _norm_block128(k[:, sl], gck_ref[:, sl])
        mk_ref[:, sl] = kn
        mkb_ref[:, sl] = kn.astype(BF16)
    v = jnp.dot(h, w_ref[:, SEG:2 * SEG], preferred_element_type=F32)
    mv_ref[...] = v
    mvb_ref[...] = v.astype(BF16)


def _memkv(mem, mg, w_kv, gck):
    b = mem.shape[0]
    blk = lambda width: pl.BlockSpec((None, N_MEM, width), lambda bi: (bi, 0, 0))
    const = lambda shape: pl.BlockSpec(shape, lambda bi: (0,) * len(shape))
    return pl.pallas_call(
        _memkv_kernel,
        out_shape=(jax.ShapeDtypeStruct((b, N_MEM, SEG), F32), jax.ShapeDtypeStruct((b, N_MEM, SEG), F32),
                   jax.ShapeDtypeStruct((b, N_MEM, SEG), BF16), jax.ShapeDtypeStruct((b, N_MEM, SEG), BF16)),
        grid=(b,),
        in_specs=[blk(D_MODEL), const((1, D_MODEL)), const((D_MODEL, 2 * SEG)), const((1, SEG))],
        out_specs=(blk(SEG), blk(SEG), blk(SEG), blk(SEG)),
        compiler_params=pltpu.CompilerParams(dimension_semantics=("arbitrary",),
                                             vmem_limit_bytes=VMEM_LIMIT),
        name="mem_kv",
    )(mem, mg, w_kv, gck)


def _attn_prompt_kernel(rel_ref, x_ref, qb_ref, kb_ref, vb_ref, cqb_ref, sgb_ref, sgc_ref, mixa_ref,
                        mkb_ref, mvb_ref, wout_ref, hg_ref, lq_ref, lk_ref, o_ref,
                        bdiag_ref, bsub_ref, qq_ref, m_ref, l_ref, acc_ref, mix_ref):
    t = ATT_TILE
    qi = pl.program_id(1)

    @pl.when((pl.program_id(0) == 0) & (qi == 0))
    def _():
        d = lax.broadcasted_iota(jnp.int32, (t, t), 0) - lax.broadcasted_iota(jnp.int32, (t, t), 1)
        for h in range(HEADS):
            bdiag_ref[h] = jnp.where(d >= 0, _shifted_bias(d, rel_ref, h), NEG_INF)
            bsub_ref[h] = _shifted_bias(d + t, rel_ref, h)

    lam = _lam(lq_ref, lk_ref)
    lo = lax.broadcasted_iota(jnp.int32, (1, HEAD_W), 1) < MAP_W
    mix_ref[:, 0:SEG] = mixa_ref[...]

    for h in range(HEADS):
        qh = qb_ref[:, h * HEAD_W:(h + 1) * HEAD_W]
        zero = jnp.zeros_like(qh)
        qq_ref[h, 0:t, :] = jnp.where(lo, qh, zero)
        qq_ref[h, t:2 * t, :] = jnp.where(lo, zero, qh)
    m_ref[...] = jnp.full(m_ref.shape, NEG_INF, F32)
    l_ref[...] = jnp.zeros(l_ref.shape, F32)
    acc_ref[...] = jnp.zeros(acc_ref.shape, F32)

    def tile_step(j0, bias_ref):
        for h in range(HEADS):
            sl = slice(h * HEAD_W, (h + 1) * HEAD_W)
            s = lax.dot_general(qq_ref[h], kb_ref[pl.ds(j0, t), sl], _NT, preferred_element_type=F32)
            if bias_ref is not None:
                bias = bias_ref[h]
                s = s + jnp.concatenate([bias, bias], axis=0)
            m_old = m_ref[h]
            m_new = jnp.maximum(m_old, jnp.max(s, axis=-1, keepdims=True))
            alpha = jnp.exp(m_old - m_new)
            p = jnp.exp(s - jnp.concatenate([m_new] * (t // HEAD_W), axis=1))
            psum = p[:, 0:HEAD_W]
            for c in range(1, t // HEAD_W):
                psum = psum + p[:, c * HEAD_W:(c + 1) * HEAD_W]
            l_ref[h] = alpha * l_ref[h] + psum
            acc_ref[h] = alpha * acc_ref[h] + jnp.dot(p.astype(BF16), vb_ref[pl.ds(j0, t), sl],
                                                      preferred_element_type=F32)
            m_ref[h] = m_new

    tile_step(pl.multiple_of(qi * t, t), bdiag_ref)

    @pl.when(qi >= 1)
    def _():
        tile_step(pl.multiple_of((qi - 1) * t, t), bsub_ref)

    def far_tile(j, carry):
        tile_step(pl.multiple_of(j * t, t), None)
        return carry

    lax.fori_loop(0, jnp.maximum(qi - 1, 0), far_tile, 0)

    for h in range(HEADS):
        sl = slice(h * HEAD_W, (h + 1) * HEAD_W)
        o = acc_ref[h] / jnp.sum(l_ref[h], axis=-1, keepdims=True)
        a = o[0:t, :] - lam * o[t:2 * t, :]
        od = _norm_block128(a, hg_ref[:, sl]) * (1.0 - LAM_INIT)
        mix_ref[:, SEG + h * HEAD_W:SEG + (h + 1) * HEAD_W] = (od * sgb_ref[:, sl].astype(F32)).astype(BF16)

        sc = lax.dot_general(cqb_ref[:, sl], mkb_ref[:, sl], _NT, preferred_element_type=F32) * CROSS_SCALE
        pc = jnp.exp(sc - jnp.max(sc, axis=-1, keepdims=True))
        c = jnp.dot(pc.astype(BF16), mvb_ref[:, sl], preferred_element_type=F32)
        c = c / jnp.sum(pc, axis=-1, keepdims=True)
        mix_ref[:, 2 * SEG + h * HEAD_W:2 * SEG + (h + 1) * HEAD_W] = (c * sgc_ref[:, sl].astype(F32)).astype(BF16)

    o_ref[...] = x_ref[...] + jnp.dot(mix_ref[...], wout_ref[...], preferred_element_type=F32)


def _attn_prompt(rel_flat, x, qb, kb, vb, cqb, sgb, sgc, mixa, mkb, mvb, w_out, hg, lq, lk):
    b, s, _ = x.shape
    t = ATT_TILE
    row = lambda width: pl.BlockSpec((None, t, width), lambda bi, i: (bi, i, 0))
    seq = lambda n: pl.BlockSpec((None, n, SEG), lambda bi, i: (bi, 0, 0))
    const = lambda shape: pl.BlockSpec(shape, lambda bi, i: (0,) * len(shape))
    return pl.pallas_call(
        _attn_prompt_kernel,
        out_shape=jax.ShapeDtypeStruct((b, s, D_MODEL), F32),
        grid=(b, s // t),
        in_specs=[pl.BlockSpec(memory_space=pltpu.SMEM),
                  row(D_MODEL), row(SEG), seq(s), seq(s), row(SEG), row(SEG), row(SEG), row(SEG),
                  seq(N_MEM), seq(N_MEM), const((3 * SEG, D_MODEL)), const((1, SEG)),
                  const((2, MAP_W)), const((2, MAP_W))],
        out_specs=row(D_MODEL),
        scratch_shapes=[pltpu.VMEM((HEADS, t, t), F32), pltpu.VMEM((HEADS, t, t), F32),
                        pltpu.VMEM((HEADS, 2 * t, HEAD_W), BF16),
                        pltpu.VMEM((HEADS, 2 * t, HEAD_W), F32), pltpu.VMEM((HEADS, 2 * t, HEAD_W), F32),
                        pltpu.VMEM((HEADS, 2 * t, HEAD_W), F32), pltpu.VMEM((t, 3 * SEG), BF16)],
        compiler_params=pltpu.CompilerParams(dimension_semantics=("arbitrary", "arbitrary"),
                                             vmem_limit_bytes=VMEM_LIMIT),
        name="attn_prompt",
    )(rel_flat, x, qb, kb, vb, cqb, sgb, sgc, mixa, mkb, mvb, w_out, hg, lq, lk)


def _inproj_sample_kernel(x_ref, st_ref, ng_ref, w_ref, gq_ref, gk_ref, gcq_ref, pw_ref, ps_ref,
                          k_ref, v_ref, pool_ref, q_ref, cq_ref, sgb_ref, sgc_ref, mixa_ref, d_ref):
    nb = st_ref.shape[1]
    nt = x_ref.shape[0] // nb
    h = _rms(x_ref[...], ng_ref[...]).astype(BF16)
    u = jnp.dot(h, w_ref[:, 0:SEG], preferred_element_type=F32)
    sga = _silu(jnp.dot(h, w_ref[:, SEG:2 * SEG], preferred_element_type=F32))

    def ext(e, sl):
        if e < POOL_PAD:
            return st_ref[e, :, sl]
        return u[(e - POOL_PAD) * nb:(e - POOL_PAD + 1) * nb, sl]

    for g, win in enumerate(POOL_WINDOWS):
        sl = slice(g * HEAD_W, (g + 1) * HEAD_W)
        for tk in range(nt):
            acc = ext(POOL_PAD + tk, sl)
            for j in range(1, win):
                acc = acc + ext(POOL_PAD + tk - j, sl)
            d_ref[tk * nb:(tk + 1) * nb, sl] = acc / float(win) - ext(POOL_PAD + tk, sl)
        y = jnp.dot(d_ref[:, sl].astype(BF16), pw_ref[g], preferred_element_type=F32) * ps_ref[:, sl]
        mixa_ref[:, sl] = y * sga[:, sl]

    keep = POOL_PAD - nt
    pool_ref[0:keep] = st_ref[nt:POOL_PAD]
    for tk in range(nt):
        pool_ref[keep + tk] = u[tk * nb:(tk + 1) * nb, :]

    _project_segments(h, w_ref, gq_ref, gk_ref, gcq_ref, k_ref, v_ref, q_ref, None, None,
                      cq_ref, sgb_ref, sgc_ref)


def _inproj_sample(x_tm, state_tm, ng, w_in, gq, gk, gcq, pw, ps):
    rows = x_tm.shape[0]
    nb = state_tm.shape[1]
    vmem = pl.BlockSpec(memory_space=pltpu.VMEM)
    mat = jax.ShapeDtypeStruct((rows, SEG), F32)
    return pl.pallas_call(
        _inproj_sample_kernel,
        out_shape=(mat, mat, jax.ShapeDtypeStruct((POOL_PAD, nb, SEG), F32), mat, mat, mat, mat, mat),
        in_specs=[vmem] * 9,
        out_specs=(vmem,) * 8,
        scratch_shapes=[pltpu.VMEM((rows, SEG), F32)],
        compiler_params=pltpu.CompilerParams(vmem_limit_bytes=VMEM_LIMIT),
        name="inproj_sample",
    )(x_tm, state_tm, ng, w_in, gq, gk, gcq, pw, ps)


def _attn_sample_kernel(pt_ref, rel_ref, q_ref, kn_ref, vn_ref, cq_ref, sgb_ref, sgc_ref, mk_ref, mv_ref,
                        hg_ref, lq_ref, lk_ref, *rest):
    npg = PAGES_PER_STEP
    k_pages = rest[0:npg]
    v_pages = rest[npg:2 * npg]
    mixd_ref, mixc_ref = rest[2 * npg:2 * npg + 2]
    blast_ref, bnew_ref, kpad_ref, vpad_ref, m_ref, l_ref, acc_ref = rest[2 * npg + 2:]
    r8 = SAMPLE_ROWS
    hr = 2 * r8
    c = pl.program_id(1)
    nc = pl.num_programs(1)

    @pl.when((pl.program_id(0) == 0) & (c == 0))
    def _():
        tok = lax.broadcasted_iota(jnp.int32, (hr, PAGE), 0) & (r8 - 1)
        key = lax.broadcasted_iota(jnp.int32, (hr, PAGE), 1)
        for h in range(HEADS):
            blast_ref[h * hr:(h + 1) * hr, :] = _shifted_bias(PAGE + tok - key, rel_ref, h)
            ok = (key <= tok) & (key < kn_ref.shape[0])
            bnew_ref[h * hr:(h + 1) * hr, :] = jnp.where(ok, _shifted_bias(tok - key, rel_ref, h), NEG_INF)
        kpad_ref[...] = jnp.zeros(kpad_ref.shape, BF16)
        vpad_ref[...] = jnp.zeros(vpad_ref.shape, BF16)

    lane = lax.broadcasted_iota(jnp.int32, (1, SEG), 1)
    q8 = q_ref[...].astype(BF16)
    zero = jnp.zeros_like(q8)
    qbd = jnp.concatenate(
        [jnp.where((lane >= hm * MAP_W) & (lane < (hm + 1) * MAP_W), q8, zero) for hm in range(2 * HEADS)],
        axis=0)

    @pl.when(c == 0)
    def _():
        m_ref[...] = jnp.full(m_ref.shape, NEG_INF, F32)
        l_ref[...] = jnp.zeros(l_ref.shape, F32)
        acc_ref[...] = jnp.zeros(acc_ref.shape, F32)

    def attend(s_parts, v_of):
        s = jnp.concatenate(s_parts, axis=1) if len(s_parts) > 1 else s_parts[0]
        m_old = m_ref[...]
        m_new = jnp.maximum(m_old, jnp.max(s, axis=-1, keepdims=True))
        alpha = jnp.exp(m_old - m_new)
        p = jnp.exp(s - m_new)
        l_ref[...] = alpha * l_ref[...] + jnp.sum(p, axis=-1, keepdims=True)
        pb = p.astype(BF16)
        heads = []
        for h in range(HEADS):
            pv = None
            for i in range(len(s_parts)):
                part = jnp.dot(pb[h * hr:(h + 1) * hr, i * PAGE:(i + 1) * PAGE], v_of(i, h),
                               preferred_element_type=F32)
                pv = part if pv is None else pv + part
            heads.append(pv)
        acc_ref[...] = alpha * acc_ref[...] + jnp.concatenate(heads, axis=0)
        m_ref[...] = m_new

    last_bias = jnp.where(c == nc - 1, blast_ref[...], 0.0)
    s_parts = []
    for i in range(npg):
        s = jnp.dot(qbd, k_pages[i][...].astype(BF16), preferred_element_type=F32)
        if i == npg - 1:
            s = s + last_bias
        s_parts.append(s)
    attend(s_parts, lambda i, h: v_pages[i][pl.ds(h, PAGE, stride=HEADS), :].astype(BF16))

    @pl.when(c == nc - 1)
    def _():
        ntok = kn_ref.shape[0]
        kpad_ref[0:ntok, :] = kn_ref[...].astype(BF16)
        vpad_ref[0:ntok, :] = vn_ref[...].astype(BF16)
        s_new = lax.dot_general(qbd, kpad_ref[...], _NT, preferred_element_type=F32) + bnew_ref[...]
        attend([s_new], lambda i, h: vpad_ref[:, h * HEAD_W:(h + 1) * HEAD_W])

        lam = _lam(lq_ref, lk_ref)
        o = acc_ref[...] / l_ref[...]
        cq8 = cq_ref[...].astype(BF16)
        for h in range(HEADS):
            sl = slice(h * HEAD_W, (h + 1) * HEAD_W)
            a = o[h * hr:h * hr + r8, :] - lam * o[h * hr + r8:(h + 1) * hr, :]
            od = _norm_block128(a, hg_ref[:, sl]) * (1.0 - LAM_INIT)
            mixd_ref[:, sl] = od * sgb_ref[:, sl]
            mk_h = mk_ref[pl.ds(h, N_MEM, stride=HEADS), :].astype(BF16)
            mv_h = mv_ref[pl.ds(h, N_MEM, stride=HEADS), :].astype(BF16)
            sc = lax.dot_general(cq8[:, sl], mk_h, _NT, preferred_element_type=F32) * CROSS_SCALE
            pc = jnp.exp(sc - jnp.max(sc, axis=-1, keepdims=True))
            oc = jnp.dot(pc.astype(BF16), mv_h, preferred_element_type=F32)
            oc = oc / jnp.sum(pc, axis=-1, keepdims=True)
            mixc_ref[:, sl] = oc * sgc_ref[:, sl]


def _attn_sample(page_table, rel_flat, q8, kn, vn, cq8, sgb8, sgc8, mem_k, mem_v, cache_kt, cache_v2, hg, lq, lk):
    nb, n_pages = page_table.shape
    npg = PAGES_PER_STEP
    r8 = SAMPLE_ROWS
    nq = HEADS * 2 * r8
    ntok = kn.shape[1]
    per_seq = lambda rows: pl.BlockSpec((None, rows, SEG), lambda b, c, pt: (b, 0, 0))
    mem = pl.BlockSpec((N_MEM * HEADS, HEAD_W), lambda b, c, pt: (b, 0))
    const = lambda shape: pl.BlockSpec(shape, lambda b, c, pt: (0,) * len(shape))

    def k_page(i):
        return pl.BlockSpec((None, SEG, PAGE), lambda b, c, pt: (pt[b, c * npg + i], 0, 0))

    def v_page(i):
        return pl.BlockSpec((PAGE * HEADS, HEAD_W), lambda b, c, pt: (pt[b, c * npg + i], 0))

    grid_spec = pltpu.PrefetchScalarGridSpec(
        num_scalar_prefetch=1,
        grid=(nb, n_pages // npg),
        in_specs=[pl.BlockSpec(memory_space=pltpu.SMEM),
                  per_seq(r8), per_seq(ntok), per_seq(ntok), per_seq(r8), per_seq(r8), per_seq(r8),
                  mem, mem, const((1, SEG)), const((2, MAP_W)), const((2, MAP_W))]
                 + [k_page(i) for i in range(npg)] + [v_page(i) for i in range(npg)],
        out_specs=(per_seq(r8), per_seq(r8)),
        scratch_shapes=[pltpu.VMEM((nq, PAGE), F32), pltpu.VMEM((nq, PAGE), F32),
                        pltpu.VMEM((PAGE, SEG), BF16), pltpu.VMEM((PAGE, SEG), BF16),
                        pltpu.VMEM((nq, 1), F32), pltpu.VMEM((nq, 1), F32), pltpu.VMEM((nq, HEAD_W), F32)],
    )
    out = jax.ShapeDtypeStruct((nb, r8, SEG), F32)
    return pl.pallas_call(
        _attn_sample_kernel,
        out_shape=(out, out),
        grid_spec=grid_spec,
        compiler_params=pltpu.CompilerParams(dimension_semantics=("arbitrary", "arbitrary"),
                                             vmem_limit_bytes=VMEM_LIMIT),
        name="attn_sample",
    )(page_table, rel_flat, q8, kn, vn, cq8, sgb8, sgc8, mem_k, mem_v, hg, lq, lk,
      *([cache_kt] * npg), *([cache_v2] * npg))


def _outproj_kernel(x_ref, a_ref, d_ref, c_ref, w_ref, o_ref):
    y = jnp.dot(a_ref[...].astype(BF16), w_ref[0:SEG, :], preferred_element_type=F32)
    y = y + jnp.dot(d_ref[...].astype(BF16), w_ref[SEG:2 * SEG, :], preferred_element_type=F32)
    y = y + jnp.dot(c_ref[...].astype(BF16), w_ref[2 * SEG:3 * SEG, :], preferred_element_type=F32)
    o_ref[...] = x_ref[...] + y


def _outproj(x, a, d, c, w_out):
    rows = x.shape[0]
    vmem = pl.BlockSpec(memory_space=pltpu.VMEM)
    return pl.pallas_call(
        _outproj_kernel,
        out_shape=jax.ShapeDtypeStruct((rows, D_MODEL), F32),
        in_specs=[vmem] * 5,
        out_specs=vmem,
        compiler_params=pltpu.CompilerParams(vmem_limit_bytes=VMEM_LIMIT),
        name="outproj_sample",
    )(x, a, d, c, w_out)


def _tile_gain(g, reps):
    return jnp.tile(g.reshape(1, -1), (1, reps))


def kernel(x_prompt, x_sample, mem_prompt, cache_k, cache_v, cache_mem_k, cache_mem_v, state_pool, page_table,
           norm_g, w_in, q_norm_g, k_norm_g, cq_norm_g, ck_norm_g, mem_norm_g, w_mem_kv, lam_q, lam_k,
           head_norm_g, pool_w, pool_scale, rel_bias, w_out):
    depth = w_in.shape[0]
    assert depth == 1
    l = 0
    b, s, _ = x_prompt.shape
    nb, nt, _ = x_sample.shape
    r8 = SAMPLE_ROWS

    ng = norm_g[l].reshape(1, D_MODEL)
    w_in_b = w_in[l].astype(BF16)
    gq = _tile_gain(q_norm_g[l], SEG // MAP_W)
    gk = _tile_gain(k_norm_g[l], SEG // MAP_W)
    gcq = _tile_gain(cq_norm_g[l], HEADS)
    gck = _tile_gain(ck_norm_g[l], HEADS)
    pw = pool_w[l].astype(BF16)
    ps = pool_scale[l].reshape(1, SEG)
    hg = head_norm_g[l].reshape(1, SEG)
    w_out_b = w_out[l].astype(BF16)
    rel_flat = rel_bias.reshape(-1)

    k_p, v_p, pool_p, qb, kb, vb, cqb, sgb, sgc, mixa = _inproj_prompt(x_prompt, ng, w_in_b, gq, gk, gcq, pw, ps)
    mk, mv, mkb, mvb = _memkv(mem_prompt, mem_norm_g[l].reshape(1, D_MODEL), w_mem_kv[l].astype(BF16), gck)
    y_p = _attn_prompt(rel_flat, x_prompt, qb, kb, vb, cqb, sgb, sgc, mixa, mkb, mvb, w_out_b, hg,
                       lam_q[l], lam_k[l])

    x_tm = x_sample.transpose(1, 0, 2).reshape(nt * nb, D_MODEL)
    st_tm = state_pool[l].transpose(1, 0, 2)
    k_s, v_s, pool_s, q_s, cq_s, sgb_s, sgc_s, mixa_s = _inproj_sample(x_tm, st_tm, ng, w_in_b, gq, gk, gcq, pw, ps)

    def seq_major(a, rows):
        a = a.reshape(nt, nb, SEG).transpose(1, 0, 2)
        return a if rows == nt else jnp.pad(a, ((0, 0), (0, rows - nt), (0, 0)))

    kn = seq_major(k_s, nt)
    vn = seq_major(v_s, nt)
    n_pool = cache_k.shape[1]
    cache_kt = jnp.transpose(cache_k[l], (0, 2, 3, 4, 1)).reshape(n_pool, SEG, PAGE)
    cache_v2 = cache_v[l].reshape(n_pool * PAGE * HEADS, HEAD_W)
    mem_k2 = cache_mem_k[l].reshape(nb * N_MEM * HEADS, HEAD_W)
    mem_v2 = cache_mem_v[l].reshape(nb * N_MEM * HEADS, HEAD_W)
    mixd, mixc = _attn_sample(
        page_table, rel_flat, seq_major(q_s, r8), kn, vn, seq_major(cq_s, r8), seq_major(sgb_s, r8),
        seq_major(sgc_s, r8), mem_k2, mem_v2, cache_kt, cache_v2, hg, lam_q[l], lam_k[l])
    x8 = jnp.pad(x_sample, ((0, 0), (0, r8 - nt), (0, 0))).reshape(nb * r8, D_MODEL)
    y8 = _outproj(x8, seq_major(mixa_s, r8).reshape(nb * r8, SEG), mixd.reshape(nb * r8, SEG),
                  mixc.reshape(nb * r8, SEG), w_out_b)
    y_s = y8.reshape(nb, r8, D_MODEL)[:, :nt]

    return (y_p, y_s,
            k_p.reshape(1, b, s, HEADS, 2, MAP_W), v_p.reshape(1, b, s, HEADS, HEAD_W),
            pool_p[None], mk.reshape(1, b, N_MEM, HEADS, HEAD_W), mv.reshape(1, b, N_MEM, HEADS, HEAD_W),
            kn.reshape(1, nb, nt, HEADS, 2, MAP_W), vn.reshape(1, nb, nt, HEADS, HEAD_W),
            pool_s.transpose(1, 0, 2)[None])
```

```python
import functools
import math

import jax
import jax.numpy as jnp
from jax import lax
from jax.experimental import pallas as pl
from jax.experimental.pallas import tpu as pltpu

F32 = jnp.float32
BF16 = jnp.bfloat16

D_MODEL = 1024
SEG = 512
N_SEG = 8
POOL_WINDOWS = (2, 4, 8, 16)
POOL_PAD = 15
HEADS = 4
HEAD_W = 128
MAP_W = 64
N_MEM = 256
PAGE = 128
NUM_BUCKETS = 32
EPS = 1e-6
NEG_INF = -1e30
LAM_INIT = 0.8 - 0.6 * math.exp(-0.3 * 0)
DIFF_SCALE = MAP_W ** -0.5
CROSS_SCALE = HEAD_W ** -0.5
LOG2E = math.log2(math.e)

ROW_TILE = 512
ATT_TILE = 256
PAGES_PER_STEP = 16
SAMPLE_STREAMS = 1
SAMPLE_ROWS = 8
VMEM_LIMIT = 48 * 1024 * 1024

_NT = (((1,), (1,)), ((), ()))


def _rms(x, g):
    ms = jnp.mean(x * x, axis=-1, keepdims=True)
    return x * lax.rsqrt(ms + EPS) * g


def _silu(x):
    return x * jax.nn.sigmoid(x)


def _norm_block64(blk, g):
    lo = lax.broadcasted_iota(jnp.int32, (1, HEAD_W), 1) < MAP_W
    sq = blk * blk
    s_lo = jnp.sum(jnp.where(lo, sq, 0.0), axis=-1, keepdims=True)
    s_hi = jnp.sum(jnp.where(lo, 0.0, sq), axis=-1, keepdims=True)
    r = jnp.where(lo, lax.rsqrt(s_lo * (1.0 / MAP_W) + EPS), lax.rsqrt(s_hi * (1.0 / MAP_W) + EPS))
    return blk * r * g


def _norm_block128(blk, g):
    ms = jnp.mean(blk * blk, axis=-1, keepdims=True)
    return blk * lax.rsqrt(ms + EPS) * g


def _lam(lq_ref, lk_ref):
    e = jnp.exp(jnp.sum(lq_ref[...] * lk_ref[...], axis=-1, keepdims=True))
    return e[0:1, :] - e[1:2, :] + LAM_INIT


def _shifted_bias(dist, rel_ref, h):
    n = jnp.maximum(dist, 0)
    nf = jnp.maximum(n, 1).astype(F32)
    large = 16 + (jnp.log(nf / 16) / math.log(128 / 16) * 16).astype(jnp.int32)
    large = jnp.minimum(large, NUM_BUCKETS - 1)
    bucket = jnp.where(n < 16, n, large)
    far = rel_ref[(NUM_BUCKETS - 1) * HEADS + h]
    out = jnp.zeros(dist.shape, F32)
    for b in range(NUM_BUCKETS - 1):
        out = jnp.where(bucket == b, rel_ref[b * HEADS + h] - far, out)
    return out


def _project_segments(h, w_ref, gq_ref, gk_ref, gcq_ref, k_ref, v_ref, qb_ref, kb_ref, vb_ref,
                      cqb_ref, sgb_ref, sgc_ref, q_scale, v_head_rows):
    rows = h.shape[0]

    def seg(s):
        return jnp.dot(h, w_ref[:, s * SEG:(s + 1) * SEG], preferred_element_type=F32)

    q = seg(2)
    for c in range(HEADS):
        sl = slice(c * HEAD_W, (c + 1) * HEAD_W)
        qb_ref[:, sl] = (_norm_block64(q[:, sl], gq_ref[:, sl]) * q_scale).astype(qb_ref.dtype)
    k = seg(3)
    for c in range(HEADS):
        sl = slice(c * HEAD_W, (c + 1) * HEAD_W)
        kn = _norm_block64(k[:, sl], gk_ref[:, sl])
        k_ref[:, sl] = kn
        if kb_ref is not None:
            kb_ref[:, sl] = kn.astype(BF16)
    v = seg(4)
    if v_head_rows:
        for c in range(HEADS):
            v_ref[pl.ds(c, rows, stride=HEADS), :] = v[:, c * HEAD_W:(c + 1) * HEAD_W]
    else:
        v_ref[...] = v
    if vb_ref is not None:
        vb_ref[...] = v.astype(BF16)
    sgb_ref[...] = _silu(seg(5)).astype(sgb_ref.dtype)
    cq = seg(6)
    for c in range(HEADS):
        sl = slice(c * HEAD_W, (c + 1) * HEAD_W)
        cqb_ref[:, sl] = _norm_block128(cq[:, sl], gcq_ref[:, sl]).astype(cqb_ref.dtype)
    sgc_ref[...] = _silu(seg(7)).astype(sgc_ref.dtype)


def _inproj_prompt_kernel(x_ref, ng_ref, w_ref, gq_ref, gk_ref, gcq_ref, pw_ref, ps_ref,
                          k_ref, v_ref, pool_ref, qb_ref, kb_ref, vb_ref, cqb_ref, sgb_ref, sgc_ref,
                          mixa_ref, uext_ref):
    tm = ROW_TILE
    i = pl.program_id(1)

    @pl.when(i == 0)
    def _():
        uext_ref[0:16, :] = jnp.zeros((16, SEG), F32)

    h = _rms(x_ref[...], ng_ref[...]).astype(BF16)
    u = jnp.dot(h, w_ref[:, 0:SEG], preferred_element_type=F32)
    uext_ref[16:16 + tm, :] = u
    sga = _silu(jnp.dot(h, w_ref[:, SEG:2 * SEG], preferred_element_type=F32))
    pos = i * tm + lax.broadcasted_iota(jnp.int32, (tm, 1), 0)
    for g, win in enumerate(POOL_WINDOWS):
        sl = slice(g * HEAD_W, (g + 1) * HEAD_W)
        ug = u[:, sl]
        acc = ug
        for j in range(1, win):
            acc = acc + uext_ref[16 - j:16 - j + tm, sl]
        cnt = jnp.minimum(pos + 1, win).astype(F32)
        d = acc / cnt - ug
        y = jnp.dot(d.astype(BF16), pw_ref[g], preferred_element_type=F32) * ps_ref[:, sl]
        mixa_ref[:, sl] = (y * sga[:, sl]).astype(BF16)

    @pl.when(i == pl.num_programs(1) - 1)
    def _():
        pool_ref[...] = uext_ref[tm + 1:tm + 16, :]

    uext_ref[0:16, :] = uext_ref[tm:tm + 16, :]

    _project_segments(h, w_ref, gq_ref, gk_ref, gcq_ref, k_ref, v_ref, qb_ref, kb_ref, vb_ref,
                      cqb_ref, sgb_ref, sgc_ref, q_scale=DIFF_SCALE * LOG2E, v_head_rows=True)


def _inproj_prompt(x, ng, w_in, gq, gk, gcq, pw, ps):
    b, s, _ = x.shape
    tm = ROW_TILE
    row = lambda width: pl.BlockSpec((None, tm, width), lambda bi, i: (bi, i, 0))
    const = lambda shape: pl.BlockSpec(shape, lambda bi, i: (0,) * len(shape))
    out_shape = (
        jax.ShapeDtypeStruct((b, s, SEG), F32),
        jax.ShapeDtypeStruct((b, s * HEADS, HEAD_W), F32),
        jax.ShapeDtypeStruct((b, POOL_PAD, SEG), F32),
        jax.ShapeDtypeStruct((b, s, SEG), BF16),
        jax.ShapeDtypeStruct((b, s, SEG), BF16),
        jax.ShapeDtypeStruct((b, s, SEG), BF16),
        jax.ShapeDtypeStruct((b, s, SEG), BF16),
        jax.ShapeDtypeStruct((b, s, SEG), BF16),
        jax.ShapeDtypeStruct((b, s, SEG), BF16),
        jax.ShapeDtypeStruct((b, s, SEG), BF16),
    )
    out_specs = (row(SEG), pl.BlockSpec((None, tm * HEADS, HEAD_W), lambda bi, i: (bi, i, 0)),
                 pl.BlockSpec((None, POOL_PAD, SEG), lambda bi, i: (bi, 0, 0)),
                 row(SEG), row(SEG), row(SEG), row(SEG), row(SEG), row(SEG), row(SEG))
    return pl.pallas_call(
        _inproj_prompt_kernel,
        out_shape=out_shape,
        grid=(b, s // tm),
        in_specs=[row(D_MODEL), const((1, D_MODEL)), const((D_MODEL, N_SEG * SEG)),
                  const((1, SEG)), const((1, SEG)), const((1, SEG)),
                  const((len(POOL_WINDOWS), HEAD_W, HEAD_W)), const((1, SEG))],
        out_specs=out_specs,
        scratch_shapes=[pltpu.VMEM((tm + 16, SEG), F32)],
        compiler_params=pltpu.CompilerParams(dimension_semantics=("arbitrary", "arbitrary"),
                                             vmem_limit_bytes=VMEM_LIMIT),
        name="inproj_prompt",
    )(x, ng, w_in, gq, gk, gcq, pw, ps)


def _memkv_kernel(mem_ref, mg_ref, w_ref, gck_ref, mk_ref, mv_ref, mkb_ref, mvb_ref):
    h = _rms(mem_ref[...], mg_ref[...]).astype(BF16)
    k = jnp.dot(h, w_ref[:, 0:SEG], preferred_element_type=F32)
    for c in range(HEADS):
        sl = slice(c * HEAD_W, (c + 1) * HEAD_W)
        kn = _norm_block128(k[:, sl], gck_ref[:, sl])
        mk_ref[:, sl] = kn
        mkb_ref[:, sl] = kn.astype(BF16)
    v = jnp.dot(h, w_ref[:, SEG:2 * SEG], preferred_element_type=F32)
    mv_ref[...] = v
    mvb_ref[...] = v.astype(BF16)


def _memkv(mem, mg, w_kv, gck):
    b = mem.shape[0]
    blk = lambda width: pl.BlockSpec((None, N_MEM, width), lambda bi: (bi, 0, 0))
    const = lambda shape: pl.BlockSpec(shape, lambda bi: (0,) * len(shape))
    return pl.pallas_call(
        _memkv_kernel,
        out_shape=(jax.ShapeDtypeStruct((b, N_MEM, SEG), F32), jax.ShapeDtypeStruct((b, N_MEM, SEG), F32),
                   jax.ShapeDtypeStruct((b, N_MEM, SEG), BF16), jax.ShapeDtypeStruct((b, N_MEM, SEG), BF16)),
        grid=(b,),
        in_specs=[blk(D_MODEL), const((1, D_MODEL)), const((D_MODEL, 2 * SEG)), const((1, SEG))],
        out_specs=(blk(SEG), blk(SEG), blk(SEG), blk(SEG)),
        compiler_params=pltpu.CompilerParams(dimension_semantics=("arbitrary",),
                                             vmem_limit_bytes=VMEM_LIMIT),
        name="mem_kv",
    )(mem, mg, w_kv, gck)


def _attn_prompt_kernel(rel_ref, x_ref, qb_ref, kb_ref, vb_ref, cqb_ref, sgb_ref, sgc_ref, mixa_ref,
                        mkb_ref, mvb_ref, wout_ref, hg_ref, lq_ref, lk_ref, o_ref,
                        bdiag_ref, bsub_ref, qq_ref, m_ref, l_ref, acc_ref, mix_ref):
    t = ATT_TILE
    qi = pl.program_id(1)

    @pl.when((pl.program_id(0) == 0) & (qi == 0))
    def _():
        d = lax.broadcasted_iota(jnp.int32, (t, t), 0) - lax.broadcasted_iota(jnp.int32, (t, t), 1)
        for h in range(HEADS):
            bdiag_ref[h] = jnp.where(d >= 0, _shifted_bias(d, rel_ref, h) * LOG2E, NEG_INF)
            bsub_ref[h] = _shifted_bias(d + t, rel_ref, h) * LOG2E

    lam = _lam(lq_ref, lk_ref)
    lo = lax.broadcasted_iota(jnp.int32, (1, HEAD_W), 1) < MAP_W
    mix_ref[:, 0:SEG] = mixa_ref[...]

    for h in range(HEADS):
        qh = qb_ref[:, h * HEAD_W:(h + 1) * HEAD_W]
        zero = jnp.zeros_like(qh)
        qq_ref[h, 0:t, :] = jnp.where(lo, qh, zero)
        qq_ref[h, t:2 * t, :] = jnp.where(lo, zero, qh)
    m_ref[...] = jnp.full(m_ref.shape, NEG_INF, F32)
    l_ref[...] = jnp.zeros(l_ref.shape, F32)
    acc_ref[...] = jnp.zeros(acc_ref.shape, F32)

    def tile_step(j0, bias_ref):
        for h in range(HEADS):
            sl = slice(h * HEAD_W, (h + 1) * HEAD_W)
            s = lax.dot_general(qq_ref[h], kb_ref[pl.ds(j0, t), sl], _NT, preferred_element_type=F32)
            if bias_ref is not None:
                bias = bias_ref[h]
                s = s + jnp.concatenate([bias, bias], axis=0)
            m_old = m_ref[h]
            m_new = jnp.maximum(m_old, jnp.max(s, axis=-1, keepdims=True))
            alpha = jnp.exp2(m_old - m_new)
            p = jnp.exp2(s - jnp.concatenate([m_new] * (t // HEAD_W), axis=1))
            psum = p[:, 0:HEAD_W]
            for c in range(1, t // HEAD_W):
                psum = psum + p[:, c * HEAD_W:(c + 1) * HEAD_W]
            l_ref[h] = alpha * l_ref[h] + psum
            acc_ref[h] = alpha * acc_ref[h] + jnp.dot(p.astype(BF16), vb_ref[pl.ds(j0, t), sl],
                                                      preferred_element_type=F32)
            m_ref[h] = m_new

    tile_step(pl.multiple_of(qi * t, t), bdiag_ref)

    @pl.when(qi >= 1)
    def _():
        tile_step(pl.multiple_of((qi - 1) * t, t), bsub_ref)

    def far_tile(j, carry):
        tile_step(pl.multiple_of(j * t, t), None)
        return carry

    lax.fori_loop(0, jnp.maximum(qi - 1, 0), far_tile, 0)

    for h in range(HEADS):
        sl = slice(h * HEAD_W, (h + 1) * HEAD_W)
        o = acc_ref[h] / jnp.sum(l_ref[h], axis=-1, keepdims=True)
        a = o[0:t, :] - lam * o[t:2 * t, :]
        od = _norm_block128(a, hg_ref[:, sl]) * (1.0 - LAM_INIT)
        mix_ref[:, SEG + h * HEAD_W:SEG + (h + 1) * HEAD_W] = (od * sgb_ref[:, sl].astype(F32)).astype(BF16)

        sc = lax.dot_general(cqb_ref[:, sl], mkb_ref[:, sl], _NT, preferred_element_type=F32) * CROSS_SCALE
        pc = jnp.exp(sc - jnp.max(sc, axis=-1, keepdims=True))
        c = jnp.dot(pc.astype(BF16), mvb_ref[:, sl], preferred_element_type=F32)
        c = c / jnp.sum(pc, axis=-1, keepdims=True)
        mix_ref[:, 2 * SEG + h * HEAD_W:2 * SEG + (h + 1) * HEAD_W] = (c * sgc_ref[:, sl].astype(F32)).astype(BF16)

    o_ref[...] = x_ref[...] + jnp.dot(mix_ref[...], wout_ref[...], preferred_element_type=F32)


def _attn_prompt(rel_flat, x, qb, kb, vb, cqb, sgb, sgc, mixa, mkb, mvb, w_out, hg, lq, lk):
    b, s, _ = x.shape
    t = ATT_TILE
    row = lambda width: pl.BlockSpec((None, t, width), lambda bi, i: (bi, i, 0))
    seq = lambda n: pl.BlockSpec((None, n, SEG), lambda bi, i: (bi, 0, 0))
    const = lambda shape: pl.BlockSpec(shape, lambda bi, i: (0,) * len(shape))
    return pl.pallas_call(
        _attn_prompt_kernel,
        out_shape=jax.ShapeDtypeStruct((b, s, D_MODEL), F32),
        grid=(b, s // t),
        in_specs=[pl.BlockSpec(memory_space=pltpu.SMEM),
                  row(D_MODEL), row(SEG), seq(s), seq(s), row(SEG), row(SEG), row(SEG), row(SEG),
                  seq(N_MEM), seq(N_MEM), const((3 * SEG, D_MODEL)), const((1, SEG)),
                  const((2, MAP_W)), const((2, MAP_W))],
        out_specs=row(D_MODEL),
        scratch_shapes=[pltpu.VMEM((HEADS, t, t), F32), pltpu.VMEM((HEADS, t, t), F32),
                        pltpu.VMEM((HEADS, 2 * t, HEAD_W), BF16),
                        pltpu.VMEM((HEADS, 2 * t, HEAD_W), F32), pltpu.VMEM((HEADS, 2 * t, HEAD_W), F32),
                        pltpu.VMEM((HEADS, 2 * t, HEAD_W), F32), pltpu.VMEM((t, 3 * SEG), BF16)],
        compiler_params=pltpu.CompilerParams(dimension_semantics=("arbitrary", "arbitrary"),
                                             vmem_limit_bytes=VMEM_LIMIT),
        name="attn_prompt",
    )(rel_flat, x, qb, kb, vb, cqb, sgb, sgc, mixa, mkb, mvb, w_out, hg, lq, lk)


def _inproj_sample_kernel(x_ref, st_ref, ng_ref, w_ref, gq_ref, gk_ref, gcq_ref, pw_ref, ps_ref,
                          k_ref, v_ref, pool_ref, q_ref, cq_ref, sgb_ref, sgc_ref, mixa_ref, d_ref):
    nb = st_ref.shape[1]
    nt = x_ref.shape[0] // nb
    h = _rms(x_ref[...], ng_ref[...]).astype(BF16)
    u = jnp.dot(h, w_ref[:, 0:SEG], preferred_element_type=F32)
    sga = _silu(jnp.dot(h, w_ref[:, SEG:2 * SEG], preferred_element_type=F32))

    def ext(e, sl):
        if e < POOL_PAD:
            return st_ref[e, :, sl]
        return u[(e - POOL_PAD) * nb:(e - POOL_PAD + 1) * nb, sl]

    for g, win in enumerate(POOL_WINDOWS):
        sl = slice(g * HEAD_W, (g + 1) * HEAD_W)
        for tk in range(nt):
            acc = ext(POOL_PAD + tk, sl)
            for j in range(1, win):
                acc = acc + ext(POOL_PAD + tk - j, sl)
            d_ref[tk * nb:(tk + 1) * nb, sl] = acc / float(win) - ext(POOL_PAD + tk, sl)
        y = jnp.dot(d_ref[:, sl].astype(BF16), pw_ref[g], preferred_element_type=F32) * ps_ref[:, sl]
        mixa_ref[:, sl] = y * sga[:, sl]

    keep = POOL_PAD - nt
    pool_ref[0:keep] = st_ref[nt:POOL_PAD]
    for tk in range(nt):
        pool_ref[keep + tk] = u[tk * nb:(tk + 1) * nb, :]

    _project_segments(h, w_ref, gq_ref, gk_ref, gcq_ref, k_ref, v_ref, q_ref, None, None,
                      cq_ref, sgb_ref, sgc_ref, q_scale=DIFF_SCALE, v_head_rows=False)


def _inproj_sample(x_tm, state_tm, ng, w_in, gq, gk, gcq, pw, ps):
    rows = x_tm.shape[0]
    nb = state_tm.shape[1]
    vmem = pl.BlockSpec(memory_space=pltpu.VMEM)
    mat = jax.ShapeDtypeStruct((rows, SEG), F32)
    return pl.pallas_call(
        _inproj_sample_kernel,
        out_shape=(mat, mat, jax.ShapeDtypeStruct((POOL_PAD, nb, SEG), F32), mat, mat, mat, mat, mat),
        in_specs=[vmem] * 9,
        out_specs=(vmem,) * 8,
        scratch_shapes=[pltpu.VMEM((rows, SEG), F32)],
        compiler_params=pltpu.CompilerParams(vmem_limit_bytes=VMEM_LIMIT),
        name="inproj_sample",
    )(x_tm, state_tm, ng, w_in, gq, gk, gcq, pw, ps)


def _attn_sample_kernel(pt_ref, rel_ref, q_ref, kn_ref, vn_ref, cq_ref, sgb_ref, sgc_ref, mk_ref, mv_ref,
                        hg_ref, lq_ref, lk_ref, *rest):
    npg = PAGES_PER_STEP
    k_pages = rest[0:npg]
    v_pages = rest[npg:2 * npg]
    mixd_ref, mixc_ref = rest[2 * npg:2 * npg + 2]
    blast_ref, bnew_ref, kpad_ref, vpad_ref, m_ref, l_ref, acc_ref = rest[2 * npg + 2:]
    r8 = SAMPLE_ROWS
    hr = 2 * r8
    c = pl.program_id(1)
    nc = pl.num_programs(1)

    @pl.when((pl.program_id(0) == 0) & (c == 0))
    def _():
        tok = lax.broadcasted_iota(jnp.int32, (hr, PAGE), 0) & (r8 - 1)
        key = lax.broadcasted_iota(jnp.int32, (hr, PAGE), 1)
        for h in range(HEADS):
            blast_ref[h * hr:(h + 1) * hr, :] = _shifted_bias(PAGE + tok - key, rel_ref, h)
            ok = (key <= tok) & (key < kn_ref.shape[0])
            bnew_ref[h * hr:(h + 1) * hr, :] = jnp.where(ok, _shifted_bias(tok - key, rel_ref, h), NEG_INF)
        kpad_ref[...] = jnp.zeros(kpad_ref.shape, BF16)
        vpad_ref[...] = jnp.zeros(vpad_ref.shape, BF16)

    lane = lax.broadcasted_iota(jnp.int32, (1, SEG), 1)
    q8 = q_ref[...].astype(BF16)
    zero = jnp.zeros_like(q8)
    qbd = jnp.concatenate(
        [jnp.where((lane >= hm * MAP_W) & (lane < (hm + 1) * MAP_W), q8, zero) for hm in range(2 * HEADS)],
        axis=0)

    @pl.when(c == 0)
    def _():
        m_ref[...] = jnp.full(m_ref.shape, NEG_INF, F32)
        l_ref[...] = jnp.zeros(l_ref.shape, F32)
        acc_ref[...] = jnp.zeros(acc_ref.shape, F32)

    def attend(g, s_parts, v_of):
        n = len(s_parts)
        s = jnp.concatenate(s_parts, axis=1) if n > 1 else s_parts[0]
        m_old = m_ref[g]
        m_new = jnp.maximum(m_old, jnp.max(s, axis=-1, keepdims=True))
        alpha = jnp.exp(m_old - m_new)
        p = jnp.exp(s - (jnp.concatenate([m_new] * n, axis=1) if n > 1 else m_new))
        psum = p[:, 0:PAGE]
        for i in range(1, n):
            psum = psum + p[:, i * PAGE:(i + 1) * PAGE]
        l_ref[g] = alpha * l_ref[g] + psum
        pb = p.astype(BF16)
        heads = []
        for h in range(HEADS):
            pv = None
            for i in range(n):
                part = jnp.dot(pb[h * hr:(h + 1) * hr, i * PAGE:(i + 1) * PAGE], v_of(i, h),
                               preferred_element_type=F32)
                pv = part if pv is None else pv + part
            heads.append(pv)
        acc_ref[g] = alpha * acc_ref[g] + jnp.concatenate(heads, axis=0)
        m_ref[g] = m_new

    last_bias = jnp.where(c == nc - 1, blast_ref[...], 0.0)
    per_stream = npg // SAMPLE_STREAMS
    for g in range(SAMPLE_STREAMS):
        s_parts = []
        for i in range(g * per_stream, (g + 1) * per_stream):
            s = jnp.dot(qbd, k_pages[i][...].astype(BF16), preferred_element_type=F32)
            if i == npg - 1:
                s = s + last_bias
            s_parts.append(s)
        attend(g, s_parts,
               lambda i, h, g=g: v_pages[g * per_stream + i][pl.ds(h, PAGE, stride=HEADS), :].astype(BF16))

    @pl.when(c == nc - 1)
    def _():
        ntok = kn_ref.shape[0]
        kpad_ref[0:ntok, :] = kn_ref[...].astype(BF16)
        vpad_ref[0:ntok, :] = vn_ref[...].astype(BF16)
        s_new = lax.dot_general(qbd, kpad_ref[...], _NT, preferred_element_type=F32) + bnew_ref[...]
        attend(0, [s_new], lambda i, h: vpad_ref[:, h * HEAD_W:(h + 1) * HEAD_W])

        m_all = m_ref[0]
        for g in range(1, SAMPLE_STREAMS):
            m_all = jnp.maximum(m_all, m_ref[g])
        l_all = None
        acc_all = None
        for g in range(SAMPLE_STREAMS):
            w = jnp.exp(m_ref[g] - m_all)
            lg = w * l_ref[g]
            ag = w * acc_ref[g]
            l_all = lg if l_all is None else l_all + lg
            acc_all = ag if acc_all is None else acc_all + ag

        lam = _lam(lq_ref, lk_ref)
        o = acc_all / jnp.sum(l_all, axis=-1, keepdims=True)
        cq8 = cq_ref[...].astype(BF16)
        for h in range(HEADS):
            sl = slice(h * HEAD_W, (h + 1) * HEAD_W)
            a = o[h * hr:h * hr + r8, :] - lam * o[h * hr + r8:(h + 1) * hr, :]
            od = _norm_block128(a, hg_ref[:, sl]) * (1.0 - LAM_INIT)
            mixd_ref[:, sl] = od * sgb_ref[:, sl]
            mk_h = mk_ref[pl.ds(h, N_MEM, stride=HEADS), :].astype(BF16)
            mv_h = mv_ref[pl.ds(h, N_MEM, stride=HEADS), :].astype(BF16)
            sc = lax.dot_general(cq8[:, sl], mk_h, _NT, preferred_element_type=F32) * CROSS_SCALE
            pc = jnp.exp(sc - jnp.max(sc, axis=-1, keepdims=True))
            oc = jnp.dot(pc.astype(BF16), mv_h, preferred_element_type=F32)
            oc = oc / jnp.sum(pc, axis=-1, keepdims=True)
            mixc_ref[:, sl] = oc * sgc_ref[:, sl]


def _attn_sample(page_table, rel_flat, q8, kn, vn, cq8, sgb8, sgc8, mem_k, mem_v, cache_kt, cache_v2, hg, lq, lk):
    nb, n_pages = page_table.shape
    npg = PAGES_PER_STEP
    r8 = SAMPLE_ROWS
    nq = HEADS * 2 * r8
    ntok = kn.shape[1]
    per_seq = lambda rows: pl.BlockSpec((None, rows, SEG), lambda b, c, pt: (b, 0, 0))
    mem = pl.BlockSpec((N_MEM * HEADS, HEAD_W), lambda b, c, pt: (b, 0))
    const = lambda shape: pl.BlockSpec(shape, lambda b, c, pt: (0,) * len(shape))

    def k_page(i):
        return pl.BlockSpec((None, SEG, PAGE), lambda b, c, pt: (pt[b, c * npg + i], 0, 0))

    def v_page(i):
        return pl.BlockSpec((PAGE * HEADS, HEAD_W), lambda b, c, pt: (pt[b, c * npg + i], 0))

    grid_spec = pltpu.PrefetchScalarGridSpec(
        num_scalar_prefetch=1,
        grid=(nb, n_pages // npg),
        in_specs=[pl.BlockSpec(memory_space=pltpu.SMEM),
                  per_seq(r8), per_seq(ntok), per_seq(ntok), per_seq(r8), per_seq(r8), per_seq(r8),
                  mem, mem, const((1, SEG)), const((2, MAP_W)), const((2, MAP_W))]
                 + [k_page(i) for i in range(npg)] + [v_page(i) for i in range(npg)],
        out_specs=(per_seq(r8), per_seq(r8)),
        scratch_shapes=[pltpu.VMEM((nq, PAGE), F32), pltpu.VMEM((nq, PAGE), F32),
                        pltpu.VMEM((PAGE, SEG), BF16), pltpu.VMEM((PAGE, SEG), BF16),
                        pltpu.VMEM((SAMPLE_STREAMS, nq, HEAD_W), F32), pltpu.VMEM((SAMPLE_STREAMS, nq, HEAD_W), F32),
                        pltpu.VMEM((SAMPLE_STREAMS, nq, HEAD_W), F32)],
    )
    out = jax.ShapeDtypeStruct((nb, r8, SEG), F32)
    return pl.pallas_call(
        _attn_sample_kernel,
        out_shape=(out, out),
        grid_spec=grid_spec,
        compiler_params=pltpu.CompilerParams(dimension_semantics=("arbitrary", "arbitrary"),
                                             vmem_limit_bytes=VMEM_LIMIT),
        name="attn_sample",
    )(page_table, rel_flat, q8, kn, vn, cq8, sgb8, sgc8, mem_k, mem_v, hg, lq, lk,
      *([cache_kt] * npg), *([cache_v2] * npg))


def _outproj_kernel(x_ref, a_ref, d_ref, c_ref, w_ref, o_ref):
    y = jnp.dot(a_ref[...].astype(BF16), w_ref[0:SEG, :], preferred_element_type=F32)
    y = y + jnp.dot(d_ref[...].astype(BF16), w_ref[SEG:2 * SEG, :], preferred_element_type=F32)
    y = y + jnp.dot(c_ref[...].astype(BF16), w_ref[2 * SEG:3 * SEG, :], preferred_element_type=F32)
    o_ref[...] = x_ref[...] + y


def _outproj(x, a, d, c, w_out):
    rows = x.shape[0]
    vmem = pl.BlockSpec(memory_space=pltpu.VMEM)
    return pl.pallas_call(
        _outproj_kernel,
        out_shape=jax.ShapeDtypeStruct((rows, D_MODEL), F32),
        in_specs=[vmem] * 5,
        out_specs=vmem,
        compiler_params=pltpu.CompilerParams(vmem_limit_bytes=VMEM_LIMIT),
        name="outproj_sample",
    )(x, a, d, c, w_out)


def _tile_gain(g, reps):
    return jnp.tile(g.reshape(1, -1), (1, reps))


def kernel(x_prompt, x_sample, mem_prompt, cache_k, cache_v, cache_mem_k, cache_mem_v, state_pool, page_table,
           norm_g, w_in, q_norm_g, k_norm_g, cq_norm_g, ck_norm_g, mem_norm_g, w_mem_kv, lam_q, lam_k,
           head_norm_g, pool_w, pool_scale, rel_bias, w_out):
    depth = w_in.shape[0]
    assert depth == 1
    l = 0
    b, s, _ = x_prompt.shape
    nb, nt, _ = x_sample.shape
    r8 = SAMPLE_ROWS

    ng = norm_g[l].reshape(1, D_MODEL)
    w_in_b = w_in[l].astype(BF16)
    gq = _tile_gain(q_norm_g[l], SEG // MAP_W)
    gk = _tile_gain(k_norm_g[l], SEG // MAP_W)
    gcq = _tile_gain(cq_norm_g[l], HEADS)
    gck = _tile_gain(ck_norm_g[l], HEADS)
    pw = pool_w[l].astype(BF16)
    ps = pool_scale[l].reshape(1, SEG)
    hg = head_norm_g[l].reshape(1, SEG)
    w_out_b = w_out[l].astype(BF16)
    rel_flat = rel_bias.reshape(-1)

    k_p, v_p, pool_p, qb, kb, vb, cqb, sgb, sgc, mixa = _inproj_prompt(x_prompt, ng, w_in_b, gq, gk, gcq, pw, ps)
    mk, mv, mkb, mvb = _memkv(mem_prompt, mem_norm_g[l].reshape(1, D_MODEL), w_mem_kv[l].astype(BF16), gck)
    y_p = _attn_prompt(rel_flat, x_prompt, qb, kb, vb, cqb, sgb, sgc, mixa, mkb, mvb, w_out_b, hg,
                       lam_q[l], lam_k[l])

    x_tm = x_sample.transpose(1, 0, 2).reshape(nt * nb, D_MODEL)
    st_tm = state_pool[l].transpose(1, 0, 2)
    k_s, v_s, pool_s, q_s, cq_s, sgb_s, sgc_s, mixa_s = _inproj_sample(x_tm, st_tm, ng, w_in_b, gq, gk, gcq, pw, ps)

    def seq_major(a, rows):
        a = a.reshape(nt, nb, SEG).transpose(1, 0, 2)
        return a if rows == nt else jnp.pad(a, ((0, 0), (0, rows - nt), (0, 0)))

    kn = seq_major(k_s, nt)
    vn = seq_major(v_s, nt)
    n_pool = cache_k.shape[1]
    cache_kt = jnp.transpose(cache_k[l], (0, 2, 3, 4, 1)).reshape(n_pool, SEG, PAGE)
    cache_v2 = cache_v[l].reshape(n_pool * PAGE * HEADS, HEAD_W)
    mem_k2 = cache_mem_k[l].reshape(nb * N_MEM * HEADS, HEAD_W)
    mem_v2 = cache_mem_v[l].reshape(nb * N_MEM * HEADS, HEAD_W)
    mixd, mixc = _attn_sample(
        page_table, rel_flat, seq_major(q_s, r8), kn, vn, seq_major(cq_s, r8), seq_major(sgb_s, r8),
        seq_major(sgc_s, r8), mem_k2, mem_v2, cache_kt, cache_v2, hg, lam_q[l], lam_k[l])
    x8 = jnp.pad(x_sample, ((0, 0), (0, r8 - nt), (0, 0))).reshape(nb * r8, D_MODEL)
    y8 = _outproj(x8, seq_major(mixa_s, r8).reshape(nb * r8, SEG), mixd.reshape(nb * r8, SEG),
                  mixc.reshape(nb * r8, SEG), w_out_b)
    y_s = y8.reshape(nb, r8, D_MODEL)[:, :nt]

    return (y_p, y_s,
            k_p.reshape(1, b, s, HEADS, 2, MAP_W), v_p.reshape(1, b, s, HEADS, HEAD_W),
            pool_p[None], mk.reshape(1, b, N_MEM, HEADS, HEAD_W), mv.reshape(1, b, N_MEM, HEADS, HEAD_W),
            kn.reshape(1, nb, nt, HEADS, 2, MAP_W), vn.reshape(1, nb, nt, HEADS, HEAD_W),
            pool_s.transpose(1, 0, 2)[None])
```

```python
import functools
import math

import jax
import jax.numpy as jnp
from jax import lax
from jax.experimental import pallas as pl
from jax.experimental.pallas import tpu as pltpu

F32 = jnp.float32
BF16 = jnp.bfloat16

D_MODEL = 1024
SEG = 512
N_SEG = 8
POOL_WINDOWS = (2, 4, 8, 16)
POOL_PAD = 15
HEADS = 4
HEAD_W = 128
MAP_W = 64
N_MEM = 256
PAGE = 128
NUM_BUCKETS = 32
EPS = 1e-6
NEG_INF = -1e30
LAM_INIT = 0.8 - 0.6 * math.exp(-0.3 * 0)
DIFF_SCALE = MAP_W ** -0.5
CROSS_SCALE = HEAD_W ** -0.5
LOG2E = math.log2(math.e)

ROW_TILE = 512
ATT_TILE = 256
SAMPLE_ROWS = 8
PAGES_PER_TILE = 8
RING = 3
SIN_ROWS = 6 * SAMPLE_ROWS
VMEM_LIMIT = 48 * 1024 * 1024
VMEM_LIMIT_ATTN = 56 * 1024 * 1024

_NT = (((1,), (1,)), ((), ()))


def _rms(x, g):
    ms = jnp.mean(x * x, axis=-1, keepdims=True)
    return x * lax.rsqrt(ms + EPS) * g


def _silu(x):
    return x * jax.nn.sigmoid(x)


def _norm_block64(blk, g):
    lo = lax.broadcasted_iota(jnp.int32, (1, HEAD_W), 1) < MAP_W
    sq = blk * blk
    s_lo = jnp.sum(jnp.where(lo, sq, 0.0), axis=-1, keepdims=True)
    s_hi = jnp.sum(jnp.where(lo, 0.0, sq), axis=-1, keepdims=True)
    r = jnp.where(lo, lax.rsqrt(s_lo * (1.0 / MAP_W) + EPS), lax.rsqrt(s_hi * (1.0 / MAP_W) + EPS))
    return blk * r * g


def _norm_block128(blk, g):
    ms = jnp.mean(blk * blk, axis=-1, keepdims=True)
    return blk * lax.rsqrt(ms + EPS) * g


def _lam(lq_ref, lk_ref):
    e = jnp.exp(jnp.sum(lq_ref[...] * lk_ref[...], axis=-1, keepdims=True))
    return e[0:1, :] - e[1:2, :] + LAM_INIT


def _shifted_bias(dist, rel_ref, h):
    n = jnp.maximum(dist, 0)
    nf = jnp.maximum(n, 1).astype(F32)
    large = 16 + (jnp.log(nf / 16) / math.log(128 / 16) * 16).astype(jnp.int32)
    large = jnp.minimum(large, NUM_BUCKETS - 1)
    bucket = jnp.where(n < 16, n, large)
    far = rel_ref[(NUM_BUCKETS - 1) * HEADS + h]
    out = jnp.zeros(dist.shape, F32)
    for b in range(NUM_BUCKETS - 1):
        out = jnp.where(bucket == b, rel_ref[b * HEADS + h] - far, out)
    return out


def _project_segments(h, w_ref, gq_ref, gk_ref, gcq_ref, k_ref, v_ref, qb_ref, kb_ref, vb_ref,
                      cqb_ref, sgb_ref, sgc_ref, q_scale, v_head_rows):
    rows = h.shape[0]

    def seg(s):
        return jnp.dot(h, w_ref[:, s * SEG:(s + 1) * SEG], preferred_element_type=F32)

    q = seg(2)
    for c in range(HEADS):
        sl = slice(c * HEAD_W, (c + 1) * HEAD_W)
        qb_ref[:, sl] = (_norm_block64(q[:, sl], gq_ref[:, sl]) * q_scale).astype(qb_ref.dtype)
    k = seg(3)
    for c in range(HEADS):
        sl = slice(c * HEAD_W, (c + 1) * HEAD_W)
        kn = _norm_block64(k[:, sl], gk_ref[:, sl])
        k_ref[:, sl] = kn
        if kb_ref is not None:
            kb_ref[:, sl] = kn.astype(BF16)
    v = seg(4)
    if v_head_rows:
        for c in range(HEADS):
            v_ref[pl.ds(c, rows, stride=HEADS), :] = v[:, c * HEAD_W:(c + 1) * HEAD_W]
    else:
        v_ref[...] = v
    if vb_ref is not None:
        vb_ref[...] = v.astype(BF16)
    sgb_ref[...] = _silu(seg(5)).astype(sgb_ref.dtype)
    cq = seg(6)
    for c in range(HEADS):
        sl = slice(c * HEAD_W, (c + 1) * HEAD_W)
        cqb_ref[:, sl] = _norm_block128(cq[:, sl], gcq_ref[:, sl]).astype(cqb_ref.dtype)
    sgc_ref[...] = _silu(seg(7)).astype(sgc_ref.dtype)


def _inproj_prompt_kernel(x_ref, ng_ref, w_ref, gq_ref, gk_ref, gcq_ref, pw_ref, ps_ref,
                          k_ref, v_ref, pool_ref, qb_ref, kb_ref, vb_ref, cqb_ref, sgb_ref, sgc_ref,
                          mixa_ref, uext_ref):
    tm = ROW_TILE
    i = pl.program_id(1)

    @pl.when(i == 0)
    def _():
        uext_ref[0:16, :] = jnp.zeros((16, SEG), F32)

    h = _rms(x_ref[...], ng_ref[...]).astype(BF16)
    u = jnp.dot(h, w_ref[:, 0:SEG], preferred_element_type=F32)
    uext_ref[16:16 + tm, :] = u
    sga = _silu(jnp.dot(h, w_ref[:, SEG:2 * SEG], preferred_element_type=F32))
    pos = i * tm + lax.broadcasted_iota(jnp.int32, (tm, 1), 0)
    for g, win in enumerate(POOL_WINDOWS):
        sl = slice(g * HEAD_W, (g + 1) * HEAD_W)
        ug = u[:, sl]
        acc = ug
        for j in range(1, win):
            acc = acc + uext_ref[16 - j:16 - j + tm, sl]
        cnt = jnp.minimum(pos + 1, win).astype(F32)
        d = acc / cnt - ug
        y = jnp.dot(d.astype(BF16), pw_ref[g], preferred_element_type=F32) * ps_ref[:, sl]
        mixa_ref[:, sl] = (y * sga[:, sl]).astype(BF16)

    @pl.when(i == pl.num_programs(1) - 1)
    def _():
        pool_ref[...] = uext_ref[tm + 1:tm + 16, :]

    uext_ref[0:16, :] = uext_ref[tm:tm + 16, :]

    _project_segments(h, w_ref, gq_ref, gk_ref, gcq_ref, k_ref, v_ref, qb_ref, kb_ref, vb_ref,
                      cqb_ref, sgb_ref, sgc_ref, q_scale=DIFF_SCALE * LOG2E, v_head_rows=True)


def _inproj_prompt(x, ng, w_in, gq, gk, gcq, pw, ps):
    b, s, _ = x.shape
    tm = ROW_TILE
    row = lambda width: pl.BlockSpec((None, tm, width), lambda bi, i: (bi, i, 0))
    const = lambda shape: pl.BlockSpec(shape, lambda bi, i: (0,) * len(shape))
    out_shape = (
        jax.ShapeDtypeStruct((b, s, SEG), F32),
        jax.ShapeDtypeStruct((b, s * HEADS, HEAD_W), F32),
        jax.ShapeDtypeStruct((b, POOL_PAD, SEG), F32),
        jax.ShapeDtypeStruct((b, s, SEG), BF16),
        jax.ShapeDtypeStruct((b, s, SEG), BF16),
        jax.ShapeDtypeStruct((b, s, SEG), BF16),
        jax.ShapeDtypeStruct((b, s, SEG), BF16),
        jax.ShapeDtypeStruct((b, s, SEG), BF16),
        jax.ShapeDtypeStruct((b, s, SEG), BF16),
        jax.ShapeDtypeStruct((b, s, SEG), BF16),
    )
    out_specs = (row(SEG), pl.BlockSpec((None, tm * HEADS, HEAD_W), lambda bi, i: (bi, i, 0)),
                 pl.BlockSpec((None, POOL_PAD, SEG), lambda bi, i: (bi, 0, 0)),
                 row(SEG), row(SEG), row(SEG), row(SEG), row(SEG), row(SEG), row(SEG))
    return pl.pallas_call(
        _inproj_prompt_kernel,
        out_shape=out_shape,
        grid=(b, s // tm),
        in_specs=[row(D_MODEL), const((1, D_MODEL)), const((D_MODEL, N_SEG * SEG)),
                  const((1, SEG)), const((1, SEG)), const((1, SEG)),
                  const((len(POOL_WINDOWS), HEAD_W, HEAD_W)), const((1, SEG))],
        out_specs=out_specs,
        scratch_shapes=[pltpu.VMEM((tm + 16, SEG), F32)],
        compiler_params=pltpu.CompilerParams(dimension_semantics=("arbitrary", "arbitrary"),
                                             vmem_limit_bytes=VMEM_LIMIT),
        name="inproj_prompt",
    )(x, ng, w_in, gq, gk, gcq, pw, ps)


def _memkv_kernel(mem_ref, mg_ref, w_ref, gck_ref, mk_ref, mv_ref, mkb_ref, mvb_ref):
    h = _rms(mem_ref[...], mg_ref[...]).astype(BF16)
    k = jnp.dot(h, w_ref[:, 0:SEG], preferred_element_type=F32)
    for c in range(HEADS):
        sl = slice(c * HEAD_W, (c + 1) * HEAD_W)
        kn = _norm_block128(k[:, sl], gck_ref[:, sl])
        mk_ref[:, sl] = kn
        mkb_ref[:, sl] = kn.astype(BF16)
    v = jnp.dot(h, w_ref[:, SEG:2 * SEG], preferred_element_type=F32)
    mv_ref[...] = v
    mvb_ref[...] = v.astype(BF16)


def _memkv(mem, mg, w_kv, gck):
    b = mem.shape[0]
    blk = lambda width: pl.BlockSpec((None, N_MEM, width), lambda bi: (bi, 0, 0))
    const = lambda shape: pl.BlockSpec(shape, lambda bi: (0,) * len(shape))
    return pl.pallas_call(
        _memkv_kernel,
        out_shape=(jax.ShapeDtypeStruct((b, N_MEM, SEG), F32), jax.ShapeDtypeStruct((b, N_MEM, SEG), F32),
                   jax.ShapeDtypeStruct((b, N_MEM, SEG), BF16), jax.ShapeDtypeStruct((b, N_MEM, SEG), BF16)),
        grid=(b,),
        in_specs=[blk(D_MODEL), const((1, D_MODEL)), const((D_MODEL, 2 * SEG)), const((1, SEG))],
        out_specs=(blk(SEG), blk(SEG), blk(SEG), blk(SEG)),
        compiler_params=pltpu.CompilerParams(dimension_semantics=("arbitrary",),
                                             vmem_limit_bytes=VMEM_LIMIT),
        name="mem_kv",
    )(mem, mg, w_kv, gck)


def _attn_kernel(pt_ref, rel_ref, x_ref, qb_ref, kb_ref, vb_ref, cqb_ref, sgb_ref, sgc_ref, mixa_ref,
                 mkb_ref, mvb_ref, wout_ref, hg_ref, lq_ref, lk_ref,
                 sin_hbm, memk_hbm, memv_hbm, kt_hbm, v_hbm,
                 o_ref, sout_hbm,
                 bdiag_ref, bsub_ref, qq_ref, m_ref, l_ref, acc_ref, mix_ref,
                 kbuf, vbuf, ring_sem, sin_buf, mk_buf, mv_buf, fetch_sem, stage, out_sem,
                 qbd_ref, sm_ref, sl_ref, sacc_ref, blast_ref, bnew_ref, kpad_ref, vpad_ref,
                 *, n_tile_steps, n_seq, n_mem_seq, n_new):
    t = ATT_TILE
    npg = PAGES_PER_TILE
    r8 = SAMPLE_ROWS
    hr = 2 * r8
    groups_per_seq = pt_ref.shape[1] // npg
    assert groups_per_seq & (groups_per_seq - 1) == 0
    seq_shift = groups_per_seq.bit_length() - 1
    bi = pl.program_id(0)
    qi = pl.program_id(1)
    nqb = kb_ref.shape[0] // t
    first_step = (bi == 0) & (qi == 0)
    last_step = (bi == pl.num_programs(0) - 1) & (qi == pl.num_programs(1) - 1)
    u_base = bi * (nqb * (nqb + 1) // 2) + jnp.right_shift(qi * (qi + 1), 1)

    lane = lax.broadcasted_iota(jnp.int32, (1, SEG), 1)

    def static(v):
        return isinstance(v, int)

    def ring_slot(pos):
        return pos % RING if static(pos) else lax.rem(pos, jnp.int32(RING))

    def clamp(v, hi):
        return min(v, hi) if static(v) else jnp.minimum(v, hi)

    def ring_copies(group, slot):
        out = []
        for i in range(npg):
            if group is None:
                pid = 0
            else:
                pid = pt_ref[jnp.right_shift(group, seq_shift), (group & (groups_per_seq - 1)) * npg + i]
            out.append(pltpu.make_async_copy(kt_hbm.at[pid], kbuf.at[slot, i], ring_sem.at[slot]))
            out.append(pltpu.make_async_copy(v_hbm.at[pid], vbuf.at[slot, i], ring_sem.at[slot]))
        return out

    def ring_start(pos):
        copies = ring_copies(jnp.minimum(pos, n_tile_steps - 1), ring_slot(pos))
        for n, cp in enumerate(copies):
            cp.start(priority=n % 2)

    def ring_wait(pos):
        for cp in ring_copies(None, ring_slot(pos)):
            cp.wait()

    def fetch_copies(seq):
        slot = seq & 1
        s_in = clamp(seq, n_seq - 1)
        s_mem = clamp(seq, n_mem_seq - 1)
        return (pltpu.make_async_copy(sin_hbm.at[s_in], sin_buf.at[slot], fetch_sem.at[slot]),
                pltpu.make_async_copy(memk_hbm.at[s_mem], mk_buf.at[slot], fetch_sem.at[slot]),
                pltpu.make_async_copy(memv_hbm.at[s_mem], mv_buf.at[slot], fetch_sem.at[slot]))

    def out_copy(row, slot):
        return pltpu.make_async_copy(stage.at[slot], sout_hbm.at[row], out_sem.at[slot])

    def sample_begin(slot):
        q8 = sin_buf[slot, 0:r8, :].astype(BF16)
        zero = jnp.zeros_like(q8)
        for hm in range(2 * HEADS):
            qbd_ref[hm * r8:(hm + 1) * r8, :] = jnp.where(
                (lane >= hm * MAP_W) & (lane < (hm + 1) * MAP_W), q8, zero)
        sm_ref[...] = jnp.full(sm_ref.shape, NEG_INF, F32)
        sl_ref[...] = jnp.zeros(sl_ref.shape, F32)
        sacc_ref[...] = jnp.zeros(sacc_ref.shape, F32)

    def sample_attend(s_parts, v_of):
        n = len(s_parts)
        s = jnp.concatenate(s_parts, axis=1) if n > 1 else s_parts[0]
        m_old = sm_ref[...]
        m_new = jnp.maximum(m_old, jnp.max(s, axis=-1, keepdims=True))
        alpha = jnp.exp(m_old - m_new)
        p = jnp.exp(s - (jnp.concatenate([m_new] * n, axis=1) if n > 1 else m_new))
        psum = p[:, 0:PAGE]
        for i in range(1, n):
            psum = psum + p[:, i * PAGE:(i + 1) * PAGE]
        sl_ref[...] = alpha * sl_ref[...] + psum
        pb = p.astype(BF16)
        heads = []
        for h in range(HEADS):
            pv = None
            for i in range(n):
                part = jnp.dot(pb[h * hr:(h + 1) * hr, i * PAGE:(i + 1) * PAGE], v_of(i, h),
                               preferred_element_type=F32)
                pv = part if pv is None else pv + part
            heads.append(pv)
        sacc_ref[...] = alpha * sacc_ref[...] + jnp.concatenate(heads, axis=0)
        sm_ref[...] = m_new

    def sample_pages(u):
        slot = lax.rem(u, RING)
        is_last_group = (u & (groups_per_seq - 1)) == groups_per_seq - 1
        last_bias = jnp.where(is_last_group, blast_ref[...], 0.0)
        qbd = qbd_ref[...]
        s_parts = []
        for i in range(npg):
            s = jnp.dot(qbd, kbuf[slot, i].astype(BF16), preferred_element_type=F32)
            if i == npg - 1:
                s = s + last_bias
            s_parts.append(s)
        sample_attend(s_parts, lambda i, h: vbuf[slot, i, pl.ds(h, PAGE, stride=HEADS), :].astype(BF16))

    def sample_end(u):
        seq = jnp.right_shift(u, seq_shift)
        slot = seq & 1
        kpad_ref[0:r8, :] = sin_buf[slot, 4 * r8:5 * r8, :].astype(BF16)
        vpad_ref[0:r8, :] = sin_buf[slot, 5 * r8:6 * r8, :].astype(BF16)
        s_new = lax.dot_general(qbd_ref[...], kpad_ref[...], _NT, preferred_element_type=F32) + bnew_ref[...]
        sample_attend([s_new], lambda i, h: vpad_ref[:, h * HEAD_W:(h + 1) * HEAD_W])

        lam = _lam(lq_ref, lk_ref)
        o = sacc_ref[...] / jnp.sum(sl_ref[...], axis=-1, keepdims=True)
        out_copy(0, slot).wait()
        cq8 = sin_buf[slot, r8:2 * r8, :].astype(BF16)
        for h in range(HEADS):
            sl = slice(h * HEAD_W, (h + 1) * HEAD_W)
            a = o[h * hr:h * hr + r8, :] - lam * o[h * hr + r8:(h + 1) * hr, :]
            od = _norm_block128(a, hg_ref[:, sl]) * (1.0 - LAM_INIT)
            stage[slot, 0:r8, sl] = od * sin_buf[slot, 2 * r8:3 * r8, sl]
            mk_h = mk_buf[slot, pl.ds(h, N_MEM, stride=HEADS), :].astype(BF16)
            mv_h = mv_buf[slot, pl.ds(h, N_MEM, stride=HEADS), :].astype(BF16)
            sc = lax.dot_general(cq8[:, sl], mk_h, _NT, preferred_element_type=F32) * CROSS_SCALE
            pc = jnp.exp(sc - jnp.max(sc, axis=-1, keepdims=True))
            oc = jnp.dot(pc.astype(BF16), mv_h, preferred_element_type=F32)
            oc = oc / jnp.sum(pc, axis=-1, keepdims=True)
            stage[slot, r8:2 * r8, sl] = oc * sin_buf[slot, 3 * r8:4 * r8, sl]
        out_copy(seq, slot).start()

        for cp in fetch_copies(seq + 1):
            cp.wait()
        sample_begin(1 - slot)
        for cp in fetch_copies(seq + 2):
            cp.start()

    @pl.when(first_step)
    def _():
        ring_start(0)
        ring_start(1)
        for cp in fetch_copies(0):
            cp.start()
        d = lax.broadcasted_iota(jnp.int32, (t, t), 0) - lax.broadcasted_iota(jnp.int32, (t, t), 1)
        for h in range(HEADS):
            bdiag_ref[h] = jnp.where(d >= 0, _shifted_bias(d, rel_ref, h) * LOG2E, NEG_INF)
            bsub_ref[h] = _shifted_bias(d + t, rel_ref, h) * LOG2E
        tok = lax.broadcasted_iota(jnp.int32, (hr, PAGE), 0) & (r8 - 1)
        key = lax.broadcasted_iota(jnp.int32, (hr, PAGE), 1)
        for h in range(HEADS):
            blast_ref[h * hr:(h + 1) * hr, :] = _shifted_bias(PAGE + tok - key, rel_ref, h)
            ok = (key <= tok) & (key < n_new)
            bnew_ref[h * hr:(h + 1) * hr, :] = jnp.where(ok, _shifted_bias(tok - key, rel_ref, h), NEG_INF)
        kpad_ref[...] = jnp.zeros(kpad_ref.shape, BF16)
        vpad_ref[...] = jnp.zeros(vpad_ref.shape, BF16)
        stage[...] = jnp.zeros(stage.shape, F32)
        out_copy(n_seq, 0).start()
        out_copy(n_seq + 1, 1).start()
        for cp in fetch_copies(0):
            cp.wait()
        sample_begin(0)
        for cp in fetch_copies(1):
            cp.start()

    lam = _lam(lq_ref, lk_ref)
    lo = lax.broadcasted_iota(jnp.int32, (1, HEAD_W), 1) < MAP_W
    mix_ref[:, 0:SEG] = mixa_ref[...]

    for h in range(HEADS):
        qh = qb_ref[:, h * HEAD_W:(h + 1) * HEAD_W]
        zero = jnp.zeros_like(qh)
        qq_ref[h, 0:t, :] = jnp.where(lo, qh, zero)
        qq_ref[h, t:2 * t, :] = jnp.where(lo, zero, qh)
    m_ref[...] = jnp.full(m_ref.shape, NEG_INF, F32)
    l_ref[...] = jnp.zeros(l_ref.shape, F32)
    acc_ref[...] = jnp.zeros(acc_ref.shape, F32)

    def tile_step(j0, bias_ref, u):
        ring_start(u + RING - 1)
        ring_wait(u)
        for h in range(HEADS):
            sl = slice(h * HEAD_W, (h + 1) * HEAD_W)
            s = lax.dot_general(qq_ref[h], kb_ref[pl.ds(j0, t), sl], _NT, preferred_element_type=F32)
            if bias_ref is not None:
                bias = bias_ref[h]
                s = s + jnp.concatenate([bias, bias], axis=0)
            m_old = m_ref[h]
            m_new = jnp.maximum(m_old, jnp.max(s, axis=-1, keepdims=True))
            alpha = jnp.exp2(m_old - m_new)
            p = jnp.exp2(s - jnp.concatenate([m_new] * (t // HEAD_W), axis=1))
            psum = p[:, 0:HEAD_W]
            for c in range(1, t // HEAD_W):
                psum = psum + p[:, c * HEAD_W:(c + 1) * HEAD_W]
            l_ref[h] = alpha * l_ref[h] + psum
            acc_ref[h] = alpha * acc_ref[h] + jnp.dot(p.astype(BF16), vb_ref[pl.ds(j0, t), sl],
                                                      preferred_element_type=F32)
            m_ref[h] = m_new
        sample_pages(u)

        @pl.when((u & (groups_per_seq - 1)) == groups_per_seq - 1)
        def _():
            sample_end(u)

    tile_step(pl.multiple_of(qi * t, t), bdiag_ref, u_base)

    @pl.when(qi >= 1)
    def _():
        tile_step(pl.multiple_of((qi - 1) * t, t), bsub_ref, u_base + 1)

    def far_tile(j, carry):
        tile_step(pl.multiple_of(j * t, t), None, u_base + 2 + j)
        return carry

    lax.fori_loop(0, jnp.maximum(qi - 1, 0), far_tile, 0)

    for h in range(HEADS):
        sl = slice(h * HEAD_W, (h + 1) * HEAD_W)
        o = acc_ref[h] / jnp.sum(l_ref[h], axis=-1, keepdims=True)
        a = o[0:t, :] - lam * o[t:2 * t, :]
        od = _norm_block128(a, hg_ref[:, sl]) * (1.0 - LAM_INIT)
        mix_ref[:, SEG + h * HEAD_W:SEG + (h + 1) * HEAD_W] = (od * sgb_ref[:, sl].astype(F32)).astype(BF16)

        sc = lax.dot_general(cqb_ref[:, sl], mkb_ref[:, sl], _NT, preferred_element_type=F32) * CROSS_SCALE
        pc = jnp.exp(sc - jnp.max(sc, axis=-1, keepdims=True))
        c = jnp.dot(pc.astype(BF16), mvb_ref[:, sl], preferred_element_type=F32)
        c = c / jnp.sum(pc, axis=-1, keepdims=True)
        mix_ref[:, 2 * SEG + h * HEAD_W:2 * SEG + (h + 1) * HEAD_W] = (c * sgc_ref[:, sl].astype(F32)).astype(BF16)

    o_ref[...] = x_ref[...] + jnp.dot(mix_ref[...], wout_ref[...], preferred_element_type=F32)

    @pl.when(last_step)
    def _():
        ring_wait(n_tile_steps)
        ring_wait(n_tile_steps + 1)
        for cp in fetch_copies(n_seq + 1):
            cp.wait()
        out_copy(0, 0).wait()
        out_copy(0, 1).wait()


def _attn(page_table, rel_flat, x, qb, kb, vb, cqb, sgb, sgc, mixa, mkb, mvb, w_out, hg, lq, lk,
          sin, mem_k3, mem_v3, cache_kt, cache_v3, n_new):
    b, s, _ = x.shape
    t = ATT_TILE
    nqb = s // t
    n_tile_steps = b * nqb * (nqb + 1) // 2
    n_seq = page_table.shape[0]
    assert n_tile_steps * PAGES_PER_TILE == n_seq * page_table.shape[1]
    assert sin.shape == (n_seq, SIN_ROWS, SEG)
    row = lambda width: pl.BlockSpec((None, t, width), lambda bi, i, pt: (bi, i, 0))
    seq = lambda n: pl.BlockSpec((None, n, SEG), lambda bi, i, pt: (bi, 0, 0))
    const = lambda shape: pl.BlockSpec(shape, lambda bi, i, pt: (0,) * len(shape))
    hbm = pl.BlockSpec(memory_space=pl.ANY)
    r8 = SAMPLE_ROWS
    nq = HEADS * 2 * r8
    grid_spec = pltpu.PrefetchScalarGridSpec(
        num_scalar_prefetch=1,
        grid=(b, nqb),
        in_specs=[pl.BlockSpec(memory_space=pltpu.SMEM),
                  row(D_MODEL), row(SEG), seq(s), seq(s), row(SEG), row(SEG), row(SEG), row(SEG),
                  seq(N_MEM), seq(N_MEM), const((3 * SEG, D_MODEL)), const((1, SEG)),
                  const((2, MAP_W)), const((2, MAP_W)), hbm, hbm, hbm, hbm, hbm],
        out_specs=(row(D_MODEL), hbm),
        scratch_shapes=[
            pltpu.VMEM((HEADS, t, t), F32), pltpu.VMEM((HEADS, t, t), F32),
            pltpu.VMEM((HEADS, 2 * t, HEAD_W), BF16),
            pltpu.VMEM((HEADS, 2 * t, HEAD_W), F32), pltpu.VMEM((HEADS, 2 * t, HEAD_W), F32),
            pltpu.VMEM((HEADS, 2 * t, HEAD_W), F32), pltpu.VMEM((t, 3 * SEG), BF16),
            pltpu.VMEM((RING, PAGES_PER_TILE, SEG, PAGE), F32),
            pltpu.VMEM((RING, PAGES_PER_TILE, PAGE * HEADS, HEAD_W), F32),
            pltpu.SemaphoreType.DMA((RING,)),
            pltpu.VMEM((2, SIN_ROWS, SEG), F32), pltpu.VMEM((2, N_MEM * HEADS, HEAD_W), F32),
            pltpu.VMEM((2, N_MEM * HEADS, HEAD_W), F32), pltpu.SemaphoreType.DMA((2,)),
            pltpu.VMEM((2, 2 * r8, SEG), F32), pltpu.SemaphoreType.DMA((2,)),
            pltpu.VMEM((nq, SEG), BF16),
            pltpu.VMEM((nq, HEAD_W), F32), pltpu.VMEM((nq, HEAD_W), F32), pltpu.VMEM((nq, HEAD_W), F32),
            pltpu.VMEM((nq, PAGE), F32), pltpu.VMEM((nq, PAGE), F32),
            pltpu.VMEM((PAGE, SEG), BF16), pltpu.VMEM((PAGE, SEG), BF16)],
    )
    kern = functools.partial(_attn_kernel, n_tile_steps=n_tile_steps, n_seq=n_seq,
                             n_mem_seq=mem_k3.shape[0], n_new=n_new)
    return pl.pallas_call(
        kern,
        out_shape=(jax.ShapeDtypeStruct((b, s, D_MODEL), F32),
                   jax.ShapeDtypeStruct((n_seq + 2, 2 * r8, SEG), F32)),
        grid_spec=grid_spec,
        compiler_params=pltpu.CompilerParams(dimension_semantics=("arbitrary", "arbitrary"),
                                             vmem_limit_bytes=VMEM_LIMIT_ATTN),
        name="attn",
    )(page_table, rel_flat, x, qb, kb, vb, cqb, sgb, sgc, mixa, mkb, mvb, w_out, hg, lq, lk,
      sin, mem_k3, mem_v3, cache_kt, cache_v3)


def _inproj_sample_kernel(x_ref, st_ref, ng_ref, w_ref, gq_ref, gk_ref, gcq_ref, pw_ref, ps_ref,
                          k_ref, v_ref, pool_ref, q_ref, cq_ref, sgb_ref, sgc_ref, mixa_ref, d_ref):
    nb = st_ref.shape[1]
    nt = x_ref.shape[0] // nb
    h = _rms(x_ref[...], ng_ref[...]).astype(BF16)
    u = jnp.dot(h, w_ref[:, 0:SEG], preferred_element_type=F32)
    sga = _silu(jnp.dot(h, w_ref[:, SEG:2 * SEG], preferred_element_type=F32))

    def ext(e, sl):
        if e < POOL_PAD:
            return st_ref[e, :, sl]
        return u[(e - POOL_PAD) * nb:(e - POOL_PAD + 1) * nb, sl]

    for g, win in enumerate(POOL_WINDOWS):
        sl = slice(g * HEAD_W, (g + 1) * HEAD_W)
        for tk in range(nt):
            acc = ext(POOL_PAD + tk, sl)
            for j in range(1, win):
                acc = acc + ext(POOL_PAD + tk - j, sl)
            d_ref[tk * nb:(tk + 1) * nb, sl] = acc / float(win) - ext(POOL_PAD + tk, sl)
        y = jnp.dot(d_ref[:, sl].astype(BF16), pw_ref[g], preferred_element_type=F32) * ps_ref[:, sl]
        mixa_ref[:, sl] = y * sga[:, sl]

    keep = POOL_PAD - nt
    pool_ref[0:keep] = st_ref[nt:POOL_PAD]
    for tk in range(nt):
        pool_ref[keep + tk] = u[tk * nb:(tk + 1) * nb, :]

    _project_segments(h, w_ref, gq_ref, gk_ref, gcq_ref, k_ref, v_ref, q_ref, None, None,
                      cq_ref, sgb_ref, sgc_ref, q_scale=DIFF_SCALE, v_head_rows=False)


def _inproj_sample(x_tm, state_tm, ng, w_in, gq, gk, gcq, pw, ps):
    rows = x_tm.shape[0]
    nb = state_tm.shape[1]
    vmem = pl.BlockSpec(memory_space=pltpu.VMEM)
    mat = jax.ShapeDtypeStruct((rows, SEG), F32)
    return pl.pallas_call(
        _inproj_sample_kernel,
        out_shape=(mat, mat, jax.ShapeDtypeStruct((POOL_PAD, nb, SEG), F32), mat, mat, mat, mat, mat),
        in_specs=[vmem] * 9,
        out_specs=(vmem,) * 8,
        scratch_shapes=[pltpu.VMEM((rows, SEG), F32)],
        compiler_params=pltpu.CompilerParams(vmem_limit_bytes=VMEM_LIMIT),
        name="inproj_sample",
    )(x_tm, state_tm, ng, w_in, gq, gk, gcq, pw, ps)


def _outproj_kernel(x_ref, a_ref, d_ref, c_ref, w_ref, o_ref):
    y = jnp.dot(a_ref[...].astype(BF16), w_ref[0:SEG, :], preferred_element_type=F32)
    y = y + jnp.dot(d_ref[...].astype(BF16), w_ref[SEG:2 * SEG, :], preferred_element_type=F32)
    y = y + jnp.dot(c_ref[...].astype(BF16), w_ref[2 * SEG:3 * SEG, :], preferred_element_type=F32)
    o_ref[...] = x_ref[...] + y


def _outproj(x, a, d, c, w_out):
    rows = x.shape[0]
    vmem = pl.BlockSpec(memory_space=pltpu.VMEM)
    return pl.pallas_call(
        _outproj_kernel,
        out_shape=jax.ShapeDtypeStruct((rows, D_MODEL), F32),
        in_specs=[vmem] * 5,
        out_specs=vmem,
        compiler_params=pltpu.CompilerParams(vmem_limit_bytes=VMEM_LIMIT),
        name="outproj_sample",
    )(x, a, d, c, w_out)


def _tile_gain(g, reps):
    return jnp.tile(g.reshape(1, -1), (1, reps))


def kernel(x_prompt, x_sample, mem_prompt, cache_k, cache_v, cache_mem_k, cache_mem_v, state_pool, page_table,
           norm_g, w_in, q_norm_g, k_norm_g, cq_norm_g, ck_norm_g, mem_norm_g, w_mem_kv, lam_q, lam_k,
           head_norm_g, pool_w, pool_scale, rel_bias, w_out):
    depth = w_in.shape[0]
    assert depth == 1
    l = 0
    b, s, _ = x_prompt.shape
    nb, nt, _ = x_sample.shape
    r8 = SAMPLE_ROWS

    ng = norm_g[l].reshape(1, D_MODEL)
    w_in_b = w_in[l].astype(BF16)
    gq = _tile_gain(q_norm_g[l], SEG // MAP_W)
    gk = _tile_gain(k_norm_g[l], SEG // MAP_W)
    gcq = _tile_gain(cq_norm_g[l], HEADS)
    gck = _tile_gain(ck_norm_g[l], HEADS)
    pw = pool_w[l].astype(BF16)
    ps = pool_scale[l].reshape(1, SEG)
    hg = head_norm_g[l].reshape(1, SEG)
    w_out_b = w_out[l].astype(BF16)
    rel_flat = rel_bias.reshape(-1)

    k_p, v_p, pool_p, qb, kb, vb, cqb, sgb, sgc, mixa = _inproj_prompt(x_prompt, ng, w_in_b, gq, gk, gcq, pw, ps)
    mk, mv, mkb, mvb = _memkv(mem_prompt, mem_norm_g[l].reshape(1, D_MODEL), w_mem_kv[l].astype(BF16), gck)
    x_tm = x_sample.transpose(1, 0, 2).reshape(nt * nb, D_MODEL)
    st_tm = state_pool[l].transpose(1, 0, 2)
    k_s, v_s, pool_s, q_s, cq_s, sgb_s, sgc_s, mixa_s = _inproj_sample(x_tm, st_tm, ng, w_in_b, gq, gk, gcq, pw, ps)

    def seq_major(a, rows):
        a = a.reshape(nt, nb, SEG).transpose(1, 0, 2)
        return a if rows == nt else jnp.pad(a, ((0, 0), (0, rows - nt), (0, 0)))

    kn = seq_major(k_s, nt)
    vn = seq_major(v_s, nt)

    nqb = s // ATT_TILE
    n_tile_steps = b * nqb * (nqb + 1) // 2
    n_pages = page_table.shape[1]
    n_seq = n_tile_steps * PAGES_PER_TILE // n_pages
    assert n_seq >= nb and n_seq * n_pages == n_tile_steps * PAGES_PER_TILE
    pt_pad = jnp.concatenate([page_table, page_table[:n_seq - nb]], axis=0)
    sin = jnp.concatenate([seq_major(q_s, r8), seq_major(cq_s, r8), seq_major(sgb_s, r8), seq_major(sgc_s, r8),
                           jnp.pad(kn, ((0, 0), (0, r8 - nt), (0, 0))), jnp.pad(vn, ((0, 0), (0, r8 - nt), (0, 0)))],
                          axis=1)
    sin = jnp.pad(sin, ((0, n_seq - nb), (0, 0), (0, 0)))
    n_pool = cache_k.shape[1]
    cache_kt = jnp.transpose(cache_k[l], (0, 2, 3, 4, 1)).reshape(n_pool, SEG, PAGE)
    cache_v3 = cache_v[l].reshape(n_pool, PAGE * HEADS, HEAD_W)
    mem_k3 = cache_mem_k[l].reshape(nb, N_MEM * HEADS, HEAD_W)
    mem_v3 = cache_mem_v[l].reshape(nb, N_MEM * HEADS, HEAD_W)
    y_p, sres = _attn(pt_pad, rel_flat, x_prompt, qb, kb, vb, cqb, sgb, sgc, mixa, mkb, mvb, w_out_b, hg,
                      lam_q[l], lam_k[l], sin, mem_k3, mem_v3, cache_kt, cache_v3, nt)
    mixd = sres[:nb, 0:r8]
    mixc = sres[:nb, r8:2 * r8]

    x8 = jnp.pad(x_sample, ((0, 0), (0, r8 - nt), (0, 0))).reshape(nb * r8, D_MODEL)
    y8 = _outproj(x8, seq_major(mixa_s, r8).reshape(nb * r8, SEG), mixd.reshape(nb * r8, SEG),
                  mixc.reshape(nb * r8, SEG), w_out_b)
    y_s = y8.reshape(nb, r8, D_MODEL)[:, :nt]

    return (y_p, y_s,
            k_p.reshape(1, b, s, HEADS, 2, MAP_W), v_p.reshape(1, b, s, HEADS, HEAD_W),
            pool_p[None], mk.reshape(1, b, N_MEM, HEADS, HEAD_W), mv.reshape(1, b, N_MEM, HEADS, HEAD_W),
            kn.reshape(1, nb, nt, HEADS, 2, MAP_W), vn.reshape(1, nb, nt, HEADS, HEAD_W),
            pool_s.transpose(1, 0, 2)[None])
```

```python
import functools
import math

import jax
import jax.numpy as jnp
from jax import lax
from jax.experimental import pallas as pl
from jax.experimental.pallas import tpu as pltpu

F32 = jnp.float32
BF16 = jnp.bfloat16

D_MODEL = 1024
SEG = 512
N_SEG = 8
POOL_WINDOWS = (2, 4, 8, 16)
POOL_PAD = 15
HEADS = 4
HEAD_W = 128
MAP_W = 64
N_MEM = 256
PAGE = 128
NUM_BUCKETS = 32
EPS = 1e-6
NEG_INF = -1e30
LAM_INIT = 0.8 - 0.6 * math.exp(-0.3 * 0)
DIFF_SCALE = MAP_W ** -0.5
CROSS_SCALE = HEAD_W ** -0.5
LOG2E = math.log2(math.e)

ROW_TILE = 512
ATT_TILE = 256
SAMPLE_ROWS = 8
PAGES_PER_TILE = 8
RING = 3
SIN_ROWS = 6 * SAMPLE_ROWS
VMEM_LIMIT = 48 * 1024 * 1024
VMEM_LIMIT_ATTN = 56 * 1024 * 1024

_NT = (((1,), (1,)), ((), ()))


def _rms(x, g):
    ms = jnp.mean(x * x, axis=-1, keepdims=True)
    return x * lax.rsqrt(ms + EPS) * g


def _silu(x):
    return x * jax.nn.sigmoid(x)


def _norm_block64(blk, g):
    lo = lax.broadcasted_iota(jnp.int32, (1, HEAD_W), 1) < MAP_W
    sq = blk * blk
    s_lo = jnp.sum(jnp.where(lo, sq, 0.0), axis=-1, keepdims=True)
    s_hi = jnp.sum(jnp.where(lo, 0.0, sq), axis=-1, keepdims=True)
    r = jnp.where(lo, lax.rsqrt(s_lo * (1.0 / MAP_W) + EPS), lax.rsqrt(s_hi * (1.0 / MAP_W) + EPS))
    return blk * r * g


def _norm_block128(blk, g):
    ms = jnp.mean(blk * blk, axis=-1, keepdims=True)
    return blk * lax.rsqrt(ms + EPS) * g


def _lam(lq_ref, lk_ref):
    e = jnp.exp(jnp.sum(lq_ref[...] * lk_ref[...], axis=-1, keepdims=True))
    return e[0:1, :] - e[1:2, :] + LAM_INIT


def _shifted_bias(dist, rel_ref, h):
    n = jnp.maximum(dist, 0)
    nf = jnp.maximum(n, 1).astype(F32)
    large = 16 + (jnp.log(nf / 16) / math.log(128 / 16) * 16).astype(jnp.int32)
    large = jnp.minimum(large, NUM_BUCKETS - 1)
    bucket = jnp.where(n < 16, n, large)
    far = rel_ref[(NUM_BUCKETS - 1) * HEADS + h]
    out = jnp.zeros(dist.shape, F32)
    for b in range(NUM_BUCKETS - 1):
        out = jnp.where(bucket == b, rel_ref[b * HEADS + h] - far, out)
    return out


def _project_segments(h, w_ref, gq_ref, gk_ref, gcq_ref, k_ref, v_ref, qb_ref, kb_ref, vb_ref,
                      cqb_ref, sgb_ref, sgc_ref, q_scale, v_head_rows):
    rows = h.shape[0]

    def seg(s):
        return jnp.dot(h, w_ref[:, s * SEG:(s + 1) * SEG], preferred_element_type=F32)

    q = seg(2)
    for c in range(HEADS):
        sl = slice(c * HEAD_W, (c + 1) * HEAD_W)
        qb_ref[:, sl] = (_norm_block64(q[:, sl], gq_ref[:, sl]) * q_scale).astype(qb_ref.dtype)
    k = seg(3)
    for c in range(HEADS):
        sl = slice(c * HEAD_W, (c + 1) * HEAD_W)
        kn = _norm_block64(k[:, sl], gk_ref[:, sl])
        k_ref[:, sl] = kn
        if kb_ref is not None:
            kb_ref[:, sl] = kn.astype(BF16)
    v = seg(4)
    if v_head_rows:
        for c in range(HEADS):
            v_ref[pl.ds(c, rows, stride=HEADS), :] = v[:, c * HEAD_W:(c + 1) * HEAD_W]
    else:
        v_ref[...] = v
    if vb_ref is not None:
        vb_ref[...] = v.astype(BF16)
    sgb_ref[...] = _silu(seg(5)).astype(sgb_ref.dtype)
    cq = seg(6)
    for c in range(HEADS):
        sl = slice(c * HEAD_W, (c + 1) * HEAD_W)
        cqb_ref[:, sl] = _norm_block128(cq[:, sl], gcq_ref[:, sl]).astype(cqb_ref.dtype)
    sgc_ref[...] = _silu(seg(7)).astype(sgc_ref.dtype)


def _inproj_prompt_kernel(x_ref, ng_ref, w_ref, gq_ref, gk_ref, gcq_ref, pw_ref, ps_ref,
                          k_ref, v_ref, pool_ref, qb_ref, kb_ref, vb_ref, cqb_ref, sgb_ref, sgc_ref,
                          mixa_ref, uext_ref):
    tm = ROW_TILE
    i = pl.program_id(1)

    @pl.when(i == 0)
    def _():
        uext_ref[0:16, :] = jnp.zeros((16, SEG), F32)

    h = _rms(x_ref[...], ng_ref[...]).astype(BF16)
    u = jnp.dot(h, w_ref[:, 0:SEG], preferred_element_type=F32)
    uext_ref[16:16 + tm, :] = u
    sga = _silu(jnp.dot(h, w_ref[:, SEG:2 * SEG], preferred_element_type=F32))
    pos = i * tm + lax.broadcasted_iota(jnp.int32, (tm, 1), 0)
    for g, win in enumerate(POOL_WINDOWS):
        sl = slice(g * HEAD_W, (g + 1) * HEAD_W)
        ug = u[:, sl]
        acc = ug
        for j in range(1, win):
            acc = acc + uext_ref[16 - j:16 - j + tm, sl]
        cnt = jnp.minimum(pos + 1, win).astype(F32)
        d = acc / cnt - ug
        y = jnp.dot(d.astype(BF16), pw_ref[g], preferred_element_type=F32) * ps_ref[:, sl]
        mixa_ref[:, sl] = (y * sga[:, sl]).astype(BF16)

    @pl.when(i == pl.num_programs(1) - 1)
    def _():
        pool_ref[...] = uext_ref[tm + 1:tm + 16, :]

    uext_ref[0:16, :] = uext_ref[tm:tm + 16, :]

    _project_segments(h, w_ref, gq_ref, gk_ref, gcq_ref, k_ref, v_ref, qb_ref, kb_ref, vb_ref,
                      cqb_ref, sgb_ref, sgc_ref, q_scale=DIFF_SCALE * LOG2E, v_head_rows=True)


def _inproj_prompt(x, ng, w_in, gq, gk, gcq, pw, ps):
    b, s, _ = x.shape
    tm = ROW_TILE
    row = lambda width: pl.BlockSpec((None, tm, width), lambda bi, i: (bi, i, 0))
    const = lambda shape: pl.BlockSpec(shape, lambda bi, i: (0,) * len(shape))
    out_shape = (
        jax.ShapeDtypeStruct((b, s, SEG), F32),
        jax.ShapeDtypeStruct((b, s * HEADS, HEAD_W), F32),
        jax.ShapeDtypeStruct((b, POOL_PAD, SEG), F32),
        jax.ShapeDtypeStruct((b, s, SEG), BF16),
        jax.ShapeDtypeStruct((b, s, SEG), BF16),
        jax.ShapeDtypeStruct((b, s, SEG), BF16),
        jax.ShapeDtypeStruct((b, s, SEG), BF16),
        jax.ShapeDtypeStruct((b, s, SEG), BF16),
        jax.ShapeDtypeStruct((b, s, SEG), BF16),
        jax.ShapeDtypeStruct((b, s, SEG), BF16),
    )
    out_specs = (row(SEG), pl.BlockSpec((None, tm * HEADS, HEAD_W), lambda bi, i: (bi, i, 0)),
                 pl.BlockSpec((None, POOL_PAD, SEG), lambda bi, i: (bi, 0, 0)),
                 row(SEG), row(SEG), row(SEG), row(SEG), row(SEG), row(SEG), row(SEG))
    return pl.pallas_call(
        _inproj_prompt_kernel,
        out_shape=out_shape,
        grid=(b, s // tm),
        in_specs=[row(D_MODEL), const((1, D_MODEL)), const((D_MODEL, N_SEG * SEG)),
                  const((1, SEG)), const((1, SEG)), const((1, SEG)),
                  const((len(POOL_WINDOWS), HEAD_W, HEAD_W)), const((1, SEG))],
        out_specs=out_specs,
        scratch_shapes=[pltpu.VMEM((tm + 16, SEG), F32)],
        compiler_params=pltpu.CompilerParams(dimension_semantics=("arbitrary", "arbitrary"),
                                             vmem_limit_bytes=VMEM_LIMIT),
        name="inproj_prompt",
    )(x, ng, w_in, gq, gk, gcq, pw, ps)


def _memkv_kernel(mem_ref, mg_ref, w_ref, gck_ref, mk_ref, mv_ref, mkb_ref, mvb_ref):
    h = _rms(mem_ref[...], mg_ref[...]).astype(BF16)
    k = jnp.dot(h, w_ref[:, 0:SEG], preferred_element_type=F32)
    for c in range(HEADS):
        sl = slice(c * HEAD_W, (c + 1) * HEAD_W)
        kn = _norm_block128(k[:, sl], gck_ref[:, sl])
        mk_ref[:, sl] = kn
        mkb_ref[:, sl] = kn.astype(BF16)
    v = jnp.dot(h, w_ref[:, SEG:2 * SEG], preferred_element_type=F32)
    mv_ref[...] = v
    mvb_ref[...] = v.astype(BF16)


def _memkv(mem, mg, w_kv, gck):
    b = mem.shape[0]
    blk = lambda width: pl.BlockSpec((None, N_MEM, width), lambda bi: (bi, 0, 0))
    const = lambda shape: pl.BlockSpec(shape, lambda bi: (0,) * len(shape))
    return pl.pallas_call(
        _memkv_kernel,
        out_shape=(jax.ShapeDtypeStruct((b, N_MEM, SEG), F32), jax.ShapeDtypeStruct((b, N_MEM, SEG), F32),
                   jax.ShapeDtypeStruct((b, N_MEM, SEG), BF16), jax.ShapeDtypeStruct((b, N_MEM, SEG), BF16)),
        grid=(b,),
        in_specs=[blk(D_MODEL), const((1, D_MODEL)), const((D_MODEL, 2 * SEG)), const((1, SEG))],
        out_specs=(blk(SEG), blk(SEG), blk(SEG), blk(SEG)),
        compiler_params=pltpu.CompilerParams(dimension_semantics=("arbitrary",),
                                             vmem_limit_bytes=VMEM_LIMIT),
        name="mem_kv",
    )(mem, mg, w_kv, gck)


def _attn_kernel(pt_ref, rel_ref, x_ref, qb_ref, kb_ref, vb_ref, cqb_ref, sgb_ref, sgc_ref, mixa_ref,
                 mkb_ref, mvb_ref, wout_ref, hg_ref, lq_ref, lk_ref,
                 sin_hbm, memk_hbm, memv_hbm, kt_hbm, v_hbm,
                 o_ref, sout_hbm,
                 bdiag_ref, bsub_ref, qq_ref, m_ref, l_ref, acc_ref, mix_ref,
                 kbuf, vbuf, ring_sem, sin_buf, mk_buf, mv_buf, fetch_sem, stage, out_sem,
                 qbd_ref, sm_ref, sl_ref, sacc_ref, blast_ref, bnew_ref, kpad_ref, vpad_ref,
                 *, n_tile_steps, n_seq, n_mem_seq, n_new):
    t = ATT_TILE
    npg = PAGES_PER_TILE
    r8 = SAMPLE_ROWS
    hr = 2 * r8
    groups_per_seq = pt_ref.shape[1] // npg
    assert groups_per_seq & (groups_per_seq - 1) == 0
    seq_shift = groups_per_seq.bit_length() - 1
    bi = pl.program_id(0)
    qi = pl.program_id(1)
    nqb = kb_ref.shape[0] // t
    first_step = (bi == 0) & (qi == 0)
    last_step = (bi == pl.num_programs(0) - 1) & (qi == pl.num_programs(1) - 1)
    u_base = bi * (nqb * (nqb + 1) // 2) + jnp.right_shift(qi * (qi + 1), 1)

    lane = lax.broadcasted_iota(jnp.int32, (1, SEG), 1)

    def static(v):
        return isinstance(v, int)

    def ring_slot(pos):
        return pos % RING if static(pos) else lax.rem(pos, jnp.int32(RING))

    def clamp(v, hi):
        return min(v, hi) if static(v) else jnp.minimum(v, hi)

    def ring_copies(group, slot):
        out = []
        for i in range(npg):
            if group is None:
                pid = 0
            else:
                pid = pt_ref[jnp.right_shift(group, seq_shift), (group & (groups_per_seq - 1)) * npg + i]
            out.append(pltpu.make_async_copy(kt_hbm.at[pid], kbuf.at[slot, i], ring_sem.at[slot]))
            out.append(pltpu.make_async_copy(v_hbm.at[pid], vbuf.at[slot, i], ring_sem.at[slot]))
        return out

    def ring_start(pos):
        copies = ring_copies(jnp.minimum(pos, n_tile_steps - 1), ring_slot(pos))
        for n, cp in enumerate(copies):
            cp.start(priority=n % 2)

    def ring_wait(pos):
        for cp in ring_copies(None, ring_slot(pos)):
            cp.wait()

    def fetch_copies(seq):
        slot = seq & 1
        s_in = clamp(seq, n_seq - 1)
        s_mem = clamp(seq, n_mem_seq - 1)
        return (pltpu.make_async_copy(sin_hbm.at[s_in], sin_buf.at[slot], fetch_sem.at[slot]),
                pltpu.make_async_copy(memk_hbm.at[s_mem], mk_buf.at[slot], fetch_sem.at[slot]),
                pltpu.make_async_copy(memv_hbm.at[s_mem], mv_buf.at[slot], fetch_sem.at[slot]))

    def out_copy(row, slot):
        return pltpu.make_async_copy(stage.at[slot], sout_hbm.at[row], out_sem.at[slot])

    def sample_begin(slot):
        q8 = sin_buf[slot, 0:r8, :].astype(BF16)
        zero = jnp.zeros_like(q8)
        for hm in range(2 * HEADS):
            qbd_ref[hm * r8:(hm + 1) * r8, :] = jnp.where(
                (lane >= hm * MAP_W) & (lane < (hm + 1) * MAP_W), q8, zero)
        sm_ref[...] = jnp.full(sm_ref.shape, NEG_INF, F32)
        sl_ref[...] = jnp.zeros(sl_ref.shape, F32)
        sacc_ref[...] = jnp.zeros(sacc_ref.shape, F32)

    def sample_attend(s_parts, v_of):
        n = len(s_parts)
        s = jnp.concatenate(s_parts, axis=1) if n > 1 else s_parts[0]
        m_old = sm_ref[...]
        m_new = jnp.maximum(m_old, jnp.max(s, axis=-1, keepdims=True))
        alpha = jnp.exp(m_old - m_new)
        p = jnp.exp(s - (jnp.concatenate([m_new] * n, axis=1) if n > 1 else m_new))
        psum = p[:, 0:PAGE]
        for i in range(1, n):
            psum = psum + p[:, i * PAGE:(i + 1) * PAGE]
        sl_ref[...] = alpha * sl_ref[...] + psum
        pb = p.astype(BF16)
        heads = []
        for h in range(HEADS):
            pv = None
            for i in range(n):
                part = jnp.dot(pb[h * hr:(h + 1) * hr, i * PAGE:(i + 1) * PAGE], v_of(i, h),
                               preferred_element_type=F32)
                pv = part if pv is None else pv + part
            heads.append(pv)
        sacc_ref[...] = alpha * sacc_ref[...] + jnp.concatenate(heads, axis=0)
        sm_ref[...] = m_new

    def sample_scores(u):
        slot = ring_slot(u)
        is_last_group = (u & (groups_per_seq - 1)) == groups_per_seq - 1
        last_bias = jnp.where(is_last_group, blast_ref[...], 0.0)
        qbd = qbd_ref[...]
        s_parts = []
        for i in range(npg):
            s = jnp.dot(qbd, kbuf[slot, i].astype(BF16), preferred_element_type=F32)
            if i == npg - 1:
                s = s + last_bias
            s_parts.append(s)
        return s_parts

    def sample_values(u, s_parts):
        slot = ring_slot(u)
        sample_attend(s_parts, lambda i, h: vbuf[slot, i, pl.ds(h, PAGE, stride=HEADS), :].astype(BF16))

    def sample_end(u):
        seq = jnp.right_shift(u, seq_shift)
        slot = seq & 1
        kpad_ref[0:r8, :] = sin_buf[slot, 4 * r8:5 * r8, :].astype(BF16)
        vpad_ref[0:r8, :] = sin_buf[slot, 5 * r8:6 * r8, :].astype(BF16)
        s_new = lax.dot_general(qbd_ref[...], kpad_ref[...], _NT, preferred_element_type=F32) + bnew_ref[...]
        sample_attend([s_new], lambda i, h: vpad_ref[:, h * HEAD_W:(h + 1) * HEAD_W])

        lam = _lam(lq_ref, lk_ref)
        o = sacc_ref[...] / jnp.sum(sl_ref[...], axis=-1, keepdims=True)
        out_copy(0, slot).wait()
        cq8 = sin_buf[slot, r8:2 * r8, :].astype(BF16)
        for h in range(HEADS):
            sl = slice(h * HEAD_W, (h + 1) * HEAD_W)
            a = o[h * hr:h * hr + r8, :] - lam * o[h * hr + r8:(h + 1) * hr, :]
            od = _norm_block128(a, hg_ref[:, sl]) * (1.0 - LAM_INIT)
            stage[slot, 0:r8, sl] = od * sin_buf[slot, 2 * r8:3 * r8, sl]
            mk_h = mk_buf[slot, pl.ds(h, N_MEM, stride=HEADS), :].astype(BF16)
            mv_h = mv_buf[slot, pl.ds(h, N_MEM, stride=HEADS), :].astype(BF16)
            sc = lax.dot_general(cq8[:, sl], mk_h, _NT, preferred_element_type=F32) * CROSS_SCALE
            pc = jnp.exp(sc - jnp.max(sc, axis=-1, keepdims=True))
            oc = jnp.dot(pc.astype(BF16), mv_h, preferred_element_type=F32)
            oc = oc / jnp.sum(pc, axis=-1, keepdims=True)
            stage[slot, r8:2 * r8, sl] = oc * sin_buf[slot, 3 * r8:4 * r8, sl]
        out_copy(seq, slot).start()

        for cp in fetch_copies(seq + 1):
            cp.wait()
        sample_begin(1 - slot)
        for cp in fetch_copies(seq + 2):
            cp.start()

    @pl.when(first_step)
    def _():
        ring_start(0)
        ring_start(1)
        for cp in fetch_copies(0):
            cp.start()
        d = lax.broadcasted_iota(jnp.int32, (t, t), 0) - lax.broadcasted_iota(jnp.int32, (t, t), 1)
        for h in range(HEADS):
            bdiag_ref[h] = jnp.where(d >= 0, _shifted_bias(d, rel_ref, h) * LOG2E, NEG_INF)
            bsub_ref[h] = _shifted_bias(d + t, rel_ref, h) * LOG2E
        tok = lax.broadcasted_iota(jnp.int32, (hr, PAGE), 0) & (r8 - 1)
        key = lax.broadcasted_iota(jnp.int32, (hr, PAGE), 1)
        for h in range(HEADS):
            blast_ref[h * hr:(h + 1) * hr, :] = _shifted_bias(PAGE + tok - key, rel_ref, h)
            ok = (key <= tok) & (key < n_new)
            bnew_ref[h * hr:(h + 1) * hr, :] = jnp.where(ok, _shifted_bias(tok - key, rel_ref, h), NEG_INF)
        kpad_ref[...] = jnp.zeros(kpad_ref.shape, BF16)
        vpad_ref[...] = jnp.zeros(vpad_ref.shape, BF16)
        stage[...] = jnp.zeros(stage.shape, F32)
        out_copy(n_seq, 0).start()
        out_copy(n_seq + 1, 1).start()
        for cp in fetch_copies(0):
            cp.wait()
        sample_begin(0)
        for cp in fetch_copies(1):
            cp.start()

    lam = _lam(lq_ref, lk_ref)
    lo = lax.broadcasted_iota(jnp.int32, (1, HEAD_W), 1) < MAP_W
    mix_ref[:, 0:SEG] = mixa_ref[...]

    for h in range(HEADS):
        qh = qb_ref[:, h * HEAD_W:(h + 1) * HEAD_W]
        zero = jnp.zeros_like(qh)
        qq_ref[h, 0:t, :] = jnp.where(lo, qh, zero)
        qq_ref[h, t:2 * t, :] = jnp.where(lo, zero, qh)
    m_ref[...] = jnp.full(m_ref.shape, NEG_INF, F32)
    l_ref[...] = jnp.zeros(l_ref.shape, F32)
    acc_ref[...] = jnp.zeros(acc_ref.shape, F32)

    def tile_step(j0, bias_ref, u):
        ring_start(u + RING - 1)
        ring_wait(u)
        s_parts = sample_scores(u)
        for h in range(HEADS):
            sl = slice(h * HEAD_W, (h + 1) * HEAD_W)
            s = lax.dot_general(qq_ref[h], kb_ref[pl.ds(j0, t), sl], _NT, preferred_element_type=F32)
            if bias_ref is not None:
                bias = bias_ref[h]
                s = s + jnp.concatenate([bias, bias], axis=0)
            m_old = m_ref[h]
            m_new = jnp.maximum(m_old, jnp.max(s, axis=-1, keepdims=True))
            alpha = jnp.exp2(m_old - m_new)
            p = jnp.exp2(s - jnp.concatenate([m_new] * (t // HEAD_W), axis=1))
            psum = p[:, 0:HEAD_W]
            for c in range(1, t // HEAD_W):
                psum = psum + p[:, c * HEAD_W:(c + 1) * HEAD_W]
            l_ref[h] = alpha * l_ref[h] + psum
            acc_ref[h] = alpha * acc_ref[h] + jnp.dot(p.astype(BF16), vb_ref[pl.ds(j0, t), sl],
                                                      preferred_element_type=F32)
            m_ref[h] = m_new
        sample_values(u, s_parts)

        @pl.when((u & (groups_per_seq - 1)) == groups_per_seq - 1)
        def _():
            sample_end(u)

    tile_step(pl.multiple_of(qi * t, t), bdiag_ref, u_base)

    @pl.when(qi >= 1)
    def _():
        tile_step(pl.multiple_of((qi - 1) * t, t), bsub_ref, u_base + 1)

    def far_tile(j, carry):
        tile_step(pl.multiple_of(j * t, t), None, u_base + 2 + j)
        return carry

    lax.fori_loop(0, jnp.maximum(qi - 1, 0), far_tile, 0)

    for h in range(HEADS):
        sl = slice(h * HEAD_W, (h + 1) * HEAD_W)
        o = acc_ref[h] / jnp.sum(l_ref[h], axis=-1, keepdims=True)
        a = o[0:t, :] - lam * o[t:2 * t, :]
        od = _norm_block128(a, hg_ref[:, sl]) * (1.0 - LAM_INIT)
        mix_ref[:, SEG + h * HEAD_W:SEG + (h + 1) * HEAD_W] = (od * sgb_ref[:, sl].astype(F32)).astype(BF16)

        sc = lax.dot_general(cqb_ref[:, sl], mkb_ref[:, sl], _NT, preferred_element_type=F32) * CROSS_SCALE
        pc = jnp.exp(sc - jnp.max(sc, axis=-1, keepdims=True))
        c = jnp.dot(pc.astype(BF16), mvb_ref[:, sl], preferred_element_type=F32)
        c = c / jnp.sum(pc, axis=-1, keepdims=True)
        mix_ref[:, 2 * SEG + h * HEAD_W:2 * SEG + (h + 1) * HEAD_W] = (c * sgc_ref[:, sl].astype(F32)).astype(BF16)

    o_ref[...] = x_ref[...] + jnp.dot(mix_ref[...], wout_ref[...], preferred_element_type=F32)

    @pl.when(last_step)
    def _():
        ring_wait(n_tile_steps)
        ring_wait(n_tile_steps + 1)
        for cp in fetch_copies(n_seq + 1):
            cp.wait()
        out_copy(0, 0).wait()
        out_copy(0, 1).wait()


def _attn(page_table, rel_flat, x, qb, kb, vb, cqb, sgb, sgc, mixa, mkb, mvb, w_out, hg, lq, lk,
          sin, mem_k3, mem_v3, cache_kt, cache_v3, n_new):
    b, s, _ = x.shape
    t = ATT_TILE
    nqb = s // t
    n_tile_steps = b * nqb * (nqb + 1) // 2
    n_seq = page_table.shape[0]
    assert n_tile_steps * PAGES_PER_TILE == n_seq * page_table.shape[1]
    assert sin.shape == (n_seq, SIN_ROWS, SEG)
    row = lambda width: pl.BlockSpec((None, t, width), lambda bi, i, pt: (bi, i, 0))
    seq = lambda n: pl.BlockSpec((None, n, SEG), lambda bi, i, pt: (bi, 0, 0))
    const = lambda shape: pl.BlockSpec(shape, lambda bi, i, pt: (0,) * len(shape))
    hbm = pl.BlockSpec(memory_space=pl.ANY)
    r8 = SAMPLE_ROWS
    nq = HEADS * 2 * r8
    grid_spec = pltpu.PrefetchScalarGridSpec(
        num_scalar_prefetch=1,
        grid=(b, nqb),
        in_specs=[pl.BlockSpec(memory_space=pltpu.SMEM),
                  row(D_MODEL), row(SEG), seq(s), seq(s), row(SEG), row(SEG), row(SEG), row(SEG),
                  seq(N_MEM), seq(N_MEM), const((3 * SEG, D_MODEL)), const((1, SEG)),
                  const((2, MAP_W)), const((2, MAP_W)), hbm, hbm, hbm, hbm, hbm],
        out_specs=(row(D_MODEL), hbm),
        scratch_shapes=[
            pltpu.VMEM((HEADS, t, t), F32), pltpu.VMEM((HEADS, t, t), F32),
            pltpu.VMEM((HEADS, 2 * t, HEAD_W), BF16),
            pltpu.VMEM((HEADS, 2 * t, HEAD_W), F32), pltpu.VMEM((HEADS, 2 * t, HEAD_W), F32),
            pltpu.VMEM((HEADS, 2 * t, HEAD_W), F32), pltpu.VMEM((t, 3 * SEG), BF16),
            pltpu.VMEM((RING, PAGES_PER_TILE, SEG, PAGE), F32),
            pltpu.VMEM((RING, PAGES_PER_TILE, PAGE * HEADS, HEAD_W), F32),
            pltpu.SemaphoreType.DMA((RING,)),
            pltpu.VMEM((2, SIN_ROWS, SEG), F32), pltpu.VMEM((2, N_MEM * HEADS, HEAD_W), F32),
            pltpu.VMEM((2, N_MEM * HEADS, HEAD_W), F32), pltpu.SemaphoreType.DMA((2,)),
            pltpu.VMEM((2, 2 * r8, SEG), F32), pltpu.SemaphoreType.DMA((2,)),
            pltpu.VMEM((nq, SEG), BF16),
            pltpu.VMEM((nq, HEAD_W), F32), pltpu.VMEM((nq, HEAD_W), F32), pltpu.VMEM((nq, HEAD_W), F32),
            pltpu.VMEM((nq, PAGE), F32), pltpu.VMEM((nq, PAGE), F32),
            pltpu.VMEM((PAGE, SEG), BF16), pltpu.VMEM((PAGE, SEG), BF16)],
    )
    kern = functools.partial(_attn_kernel, n_tile_steps=n_tile_steps, n_seq=n_seq,
                             n_mem_seq=mem_k3.shape[0], n_new=n_new)
    return pl.pallas_call(
        kern,
        out_shape=(jax.ShapeDtypeStruct((b, s, D_MODEL), F32),
                   jax.ShapeDtypeStruct((n_seq + 2, 2 * r8, SEG), F32)),
        grid_spec=grid_spec,
        compiler_params=pltpu.CompilerParams(dimension_semantics=("arbitrary", "arbitrary"),
                                             vmem_limit_bytes=VMEM_LIMIT_ATTN),
        name="attn",
    )(page_table, rel_flat, x, qb, kb, vb, cqb, sgb, sgc, mixa, mkb, mvb, w_out, hg, lq, lk,
      sin, mem_k3, mem_v3, cache_kt, cache_v3)


def _inproj_sample_kernel(x_ref, st_ref, ng_ref, w_ref, gq_ref, gk_ref, gcq_ref, pw_ref, ps_ref,
                          k_ref, v_ref, pool_ref, q_ref, cq_ref, sgb_ref, sgc_ref, mixa_ref, d_ref):
    nb = st_ref.shape[1]
    nt = x_ref.shape[0] // nb
    h = _rms(x_ref[...], ng_ref[...]).astype(BF16)
    u = jnp.dot(h, w_ref[:, 0:SEG], preferred_element_type=F32)
    sga = _silu(jnp.dot(h, w_ref[:, SEG:2 * SEG], preferred_element_type=F32))

    def ext(e, sl):
        if e < POOL_PAD:
            return st_ref[e, :, sl]
        return u[(e - POOL_PAD) * nb:(e - POOL_PAD + 1) * nb, sl]

    for g, win in enumerate(POOL_WINDOWS):
        sl = slice(g * HEAD_W, (g + 1) * HEAD_W)
        for tk in range(nt):
            acc = ext(POOL_PAD + tk, sl)
            for j in range(1, win):
                acc = acc + ext(POOL_PAD + tk - j, sl)
            d_ref[tk * nb:(tk + 1) * nb, sl] = acc / float(win) - ext(POOL_PAD + tk, sl)
        y = jnp.dot(d_ref[:, sl].astype(BF16), pw_ref[g], preferred_element_type=F32) * ps_ref[:, sl]
        mixa_ref[:, sl] = y * sga[:, sl]

    keep = POOL_PAD - nt
    pool_ref[0:keep] = st_ref[nt:POOL_PAD]
    for tk in range(nt):
        pool_ref[keep + tk] = u[tk * nb:(tk + 1) * nb, :]

    _project_segments(h, w_ref, gq_ref, gk_ref, gcq_ref, k_ref, v_ref, q_ref, None, None,
                      cq_ref, sgb_ref, sgc_ref, q_scale=DIFF_SCALE, v_head_rows=False)


def _inproj_sample(x_tm, state_tm, ng, w_in, gq, gk, gcq, pw, ps):
    rows = x_tm.shape[0]
    nb = state_tm.shape[1]
    vmem = pl.BlockSpec(memory_space=pltpu.VMEM)
    mat = jax.ShapeDtypeStruct((rows, SEG), F32)
    return pl.pallas_call(
        _inproj_sample_kernel,
        out_shape=(mat, mat, jax.ShapeDtypeStruct((POOL_PAD, nb, SEG), F32), mat, mat, mat, mat, mat),
        in_specs=[vmem] * 9,
        out_specs=(vmem,) * 8,
        scratch_shapes=[pltpu.VMEM((rows, SEG), F32)],
        compiler_params=pltpu.CompilerParams(vmem_limit_bytes=VMEM_LIMIT),
        name="inproj_sample",
    )(x_tm, state_tm, ng, w_in, gq, gk, gcq, pw, ps)


def _outproj_kernel(x_ref, a_ref, d_ref, c_ref, w_ref, o_ref):
    y = jnp.dot(a_ref[...].astype(BF16), w_ref[0:SEG, :], preferred_element_type=F32)
    y = y + jnp.dot(d_ref[...].astype(BF16), w_ref[SEG:2 * SEG, :], preferred_element_type=F32)
    y = y + jnp.dot(c_ref[...].astype(BF16), w_ref[2 * SEG:3 * SEG, :], preferred_element_type=F32)
    o_ref[...] = x_ref[...] + y


def _outproj(x, a, d, c, w_out):
    rows = x.shape[0]
    vmem = pl.BlockSpec(memory_space=pltpu.VMEM)
    return pl.pallas_call(
        _outproj_kernel,
        out_shape=jax.ShapeDtypeStruct((rows, D_MODEL), F32),
        in_specs=[vmem] * 5,
        out_specs=vmem,
        compiler_params=pltpu.CompilerParams(vmem_limit_bytes=VMEM_LIMIT),
        name="outproj_sample",
    )(x, a, d, c, w_out)


def _tile_gain(g, reps):
    return jnp.tile(g.reshape(1, -1), (1, reps))


def kernel(x_prompt, x_sample, mem_prompt, cache_k, cache_v, cache_mem_k, cache_mem_v, state_pool, page_table,
           norm_g, w_in, q_norm_g, k_norm_g, cq_norm_g, ck_norm_g, mem_norm_g, w_mem_kv, lam_q, lam_k,
           head_norm_g, pool_w, pool_scale, rel_bias, w_out):
    depth = w_in.shape[0]
    assert depth == 1
    l = 0
    b, s, _ = x_prompt.shape
    nb, nt, _ = x_sample.shape
    r8 = SAMPLE_ROWS

    ng = norm_g[l].reshape(1, D_MODEL)
    w_in_b = w_in[l].astype(BF16)
    gq = _tile_gain(q_norm_g[l], SEG // MAP_W)
    gk = _tile_gain(k_norm_g[l], SEG // MAP_W)
    gcq = _tile_gain(cq_norm_g[l], HEADS)
    gck = _tile_gain(ck_norm_g[l], HEADS)
    pw = pool_w[l].astype(BF16)
    ps = pool_scale[l].reshape(1, SEG)
    hg = head_norm_g[l].reshape(1, SEG)
    w_out_b = w_out[l].astype(BF16)
    rel_flat = rel_bias.reshape(-1)

    k_p, v_p, pool_p, qb, kb, vb, cqb, sgb, sgc, mixa = _inproj_prompt(x_prompt, ng, w_in_b, gq, gk, gcq, pw, ps)
    mk, mv, mkb, mvb = _memkv(mem_prompt, mem_norm_g[l].reshape(1, D_MODEL), w_mem_kv[l].astype(BF16), gck)
    x_tm = x_sample.transpose(1, 0, 2).reshape(nt * nb, D_MODEL)
    st_tm = state_pool[l].transpose(1, 0, 2)
    k_s, v_s, pool_s, q_s, cq_s, sgb_s, sgc_s, mixa_s = _inproj_sample(x_tm, st_tm, ng, w_in_b, gq, gk, gcq, pw, ps)

    def seq_major(a, rows):
        a = a.reshape(nt, nb, SEG).transpose(1, 0, 2)
        return a if rows == nt else jnp.pad(a, ((0, 0), (0, rows - nt), (0, 0)))

    kn = seq_major(k_s, nt)
    vn = seq_major(v_s, nt)

    nqb = s // ATT_TILE
    n_tile_steps = b * nqb * (nqb + 1) // 2
    n_pages = page_table.shape[1]
    n_seq = n_tile_steps * PAGES_PER_TILE // n_pages
    assert n_seq >= nb and n_seq * n_pages == n_tile_steps * PAGES_PER_TILE
    pt_pad = jnp.concatenate([page_table, page_table[:n_seq - nb]], axis=0)
    sin = jnp.concatenate([seq_major(q_s, r8), seq_major(cq_s, r8), seq_major(sgb_s, r8), seq_major(sgc_s, r8),
                           jnp.pad(kn, ((0, 0), (0, r8 - nt), (0, 0))), jnp.pad(vn, ((0, 0), (0, r8 - nt), (0, 0)))],
                          axis=1)
    sin = jnp.pad(sin, ((0, n_seq - nb), (0, 0), (0, 0)))
    n_pool = cache_k.shape[1]
    cache_kt = jnp.transpose(cache_k[l], (0, 2, 3, 4, 1)).reshape(n_pool, SEG, PAGE)
    cache_v3 = cache_v[l].reshape(n_pool, PAGE * HEADS, HEAD_W)
    mem_k3 = cache_mem_k[l].reshape(nb, N_MEM * HEADS, HEAD_W)
    mem_v3 = cache_mem_v[l].reshape(nb, N_MEM * HEADS, HEAD_W)
    y_p, sres = _attn(pt_pad, rel_flat, x_prompt, qb, kb, vb, cqb, sgb, sgc, mixa, mkb, mvb, w_out_b, hg,
                      lam_q[l], lam_k[l], sin, mem_k3, mem_v3, cache_kt, cache_v3, nt)
    mixd = sres[:nb, 0:r8]
    mixc = sres[:nb, r8:2 * r8]

    x8 = jnp.pad(x_sample, ((0, 0), (0, r8 - nt), (0, 0))).reshape(nb * r8, D_MODEL)
    y8 = _outproj(x8, seq_major(mixa_s, r8).reshape(nb * r8, SEG), mixd.reshape(nb * r8, SEG),
                  mixc.reshape(nb * r8, SEG), w_out_b)
    y_s = y8.reshape(nb, r8, D_MODEL)[:, :nt]

    return (y_p, y_s,
            k_p.reshape(1, b, s, HEADS, 2, MAP_W), v_p.reshape(1, b, s, HEADS, HEAD_W),
            pool_p[None], mk.reshape(1, b, N_MEM, HEADS, HEAD_W), mv.reshape(1, b, N_MEM, HEADS, HEAD_W),
            kn.reshape(1, nb, nt, HEADS, 2, MAP_W), vn.reshape(1, nb, nt, HEADS, HEAD_W),
            pool_s.transpose(1, 0, 2)[None])
```

```python
import functools
import math

import jax
import jax.numpy as jnp
from jax import lax
from jax.experimental import pallas as pl
from jax.experimental.pallas import tpu as pltpu

F32 = jnp.float32
BF16 = jnp.bfloat16

D_MODEL = 1024
SEG = 512
N_SEG = 8
POOL_WINDOWS = (2, 4, 8, 16)
POOL_PAD = 15
HEADS = 4
HEAD_W = 128
MAP_W = 64
N_MEM = 256
PAGE = 128
NUM_BUCKETS = 32
EPS = 1e-6
NEG_INF = -1e30
LAM_INIT = 0.8 - 0.6 * math.exp(-0.3 * 0)
DIFF_SCALE = MAP_W ** -0.5
CROSS_SCALE = HEAD_W ** -0.5
LOG2E = math.log2(math.e)

ROW_TILE = 512
ATT_TILE = 256
SAMPLE_ROWS = 8
PAGES_PER_TILE = 8
LOOKAHEAD = 3
RING = LOOKAHEAD + 2
SIN_ROWS = 6 * SAMPLE_ROWS
VMEM_LIMIT = 48 * 1024 * 1024
VMEM_LIMIT_ATTN = 56 * 1024 * 1024

_NT = (((1,), (1,)), ((), ()))


def _rms(x, g):
    ms = jnp.mean(x * x, axis=-1, keepdims=True)
    return x * lax.rsqrt(ms + EPS) * g


def _silu(x):
    return x * jax.nn.sigmoid(x)


def _norm_block64(blk, g):
    lo = lax.broadcasted_iota(jnp.int32, (1, HEAD_W), 1) < MAP_W
    sq = blk * blk
    s_lo = jnp.sum(jnp.where(lo, sq, 0.0), axis=-1, keepdims=True)
    s_hi = jnp.sum(jnp.where(lo, 0.0, sq), axis=-1, keepdims=True)
    r = jnp.where(lo, lax.rsqrt(s_lo * (1.0 / MAP_W) + EPS), lax.rsqrt(s_hi * (1.0 / MAP_W) + EPS))
    return blk * r * g


def _norm_block128(blk, g):
    ms = jnp.mean(blk * blk, axis=-1, keepdims=True)
    return blk * lax.rsqrt(ms + EPS) * g


def _lam(lq_ref, lk_ref):
    e = jnp.exp(jnp.sum(lq_ref[...] * lk_ref[...], axis=-1, keepdims=True))
    return e[0:1, :] - e[1:2, :] + LAM_INIT


def _shifted_bias(dist, rel_ref, h):
    n = jnp.maximum(dist, 0)
    nf = jnp.maximum(n, 1).astype(F32)
    large = 16 + (jnp.log(nf / 16) / math.log(128 / 16) * 16).astype(jnp.int32)
    large = jnp.minimum(large, NUM_BUCKETS - 1)
    bucket = jnp.where(n < 16, n, large)
    far = rel_ref[(NUM_BUCKETS - 1) * HEADS + h]
    out = jnp.zeros(dist.shape, F32)
    for b in range(NUM_BUCKETS - 1):
        out = jnp.where(bucket == b, rel_ref[b * HEADS + h] - far, out)
    return out


def _project_segments(h, w_ref, gq_ref, gk_ref, gcq_ref, k_ref, v_ref, qb_ref, kb_ref, vb_ref,
                      cqb_ref, sgb_ref, sgc_ref, q_scale, v_head_rows):
    rows = h.shape[0]

    def seg(s):
        return jnp.dot(h, w_ref[:, s * SEG:(s + 1) * SEG], preferred_element_type=F32)

    q = seg(2)
    for c in range(HEADS):
        sl = slice(c * HEAD_W, (c + 1) * HEAD_W)
        qb_ref[:, sl] = (_norm_block64(q[:, sl], gq_ref[:, sl]) * q_scale).astype(qb_ref.dtype)
    k = seg(3)
    for c in range(HEADS):
        sl = slice(c * HEAD_W, (c + 1) * HEAD_W)
        kn = _norm_block64(k[:, sl], gk_ref[:, sl])
        k_ref[:, sl] = kn
        if kb_ref is not None:
            kb_ref[:, sl] = kn.astype(BF16)
    v = seg(4)
    if v_head_rows:
        for c in range(HEADS):
            v_ref[pl.ds(c, rows, stride=HEADS), :] = v[:, c * HEAD_W:(c + 1) * HEAD_W]
    else:
        v_ref[...] = v
    if vb_ref is not None:
        vb_ref[...] = v.astype(BF16)
    sgb_ref[...] = _silu(seg(5)).astype(sgb_ref.dtype)
    cq = seg(6)
    for c in range(HEADS):
        sl = slice(c * HEAD_W, (c + 1) * HEAD_W)
        cqb_ref[:, sl] = _norm_block128(cq[:, sl], gcq_ref[:, sl]).astype(cqb_ref.dtype)
    sgc_ref[...] = _silu(seg(7)).astype(sgc_ref.dtype)


def _inproj_prompt_kernel(x_ref, ng_ref, w_ref, gq_ref, gk_ref, gcq_ref, pw_ref, ps_ref,
                          k_ref, v_ref, pool_ref, qb_ref, kb_ref, vb_ref, cqb_ref, sgb_ref, sgc_ref,
                          mixa_ref, uext_ref):
    tm = ROW_TILE
    i = pl.program_id(1)

    @pl.when(i == 0)
    def _():
        uext_ref[0:16, :] = jnp.zeros((16, SEG), F32)

    h = _rms(x_ref[...], ng_ref[...]).astype(BF16)
    u = jnp.dot(h, w_ref[:, 0:SEG], preferred_element_type=F32)
    uext_ref[16:16 + tm, :] = u
    sga = _silu(jnp.dot(h, w_ref[:, SEG:2 * SEG], preferred_element_type=F32))
    pos = i * tm + lax.broadcasted_iota(jnp.int32, (tm, 1), 0)
    for g, win in enumerate(POOL_WINDOWS):
        sl = slice(g * HEAD_W, (g + 1) * HEAD_W)
        ug = u[:, sl]
        acc = ug
        for j in range(1, win):
            acc = acc + uext_ref[16 - j:16 - j + tm, sl]
        cnt = jnp.minimum(pos + 1, win).astype(F32)
        d = acc / cnt - ug
        y = jnp.dot(d.astype(BF16), pw_ref[g], preferred_element_type=F32) * ps_ref[:, sl]
        mixa_ref[:, sl] = (y * sga[:, sl]).astype(BF16)

    @pl.when(i == pl.num_programs(1) - 1)
    def _():
        pool_ref[...] = uext_ref[tm + 1:tm + 16, :]

    uext_ref[0:16, :] = uext_ref[tm:tm + 16, :]

    _project_segments(h, w_ref, gq_ref, gk_ref, gcq_ref, k_ref, v_ref, qb_ref, kb_ref, vb_ref,
                      cqb_ref, sgb_ref, sgc_ref, q_scale=DIFF_SCALE * LOG2E, v_head_rows=True)


def _inproj_prompt(x, ng, w_in, gq, gk, gcq, pw, ps):
    b, s, _ = x.shape
    tm = ROW_TILE
    row = lambda width: pl.BlockSpec((None, tm, width), lambda bi, i: (bi, i, 0))
    const = lambda shape: pl.BlockSpec(shape, lambda bi, i: (0,) * len(shape))
    out_shape = (
        jax.ShapeDtypeStruct((b, s, SEG), F32),
        jax.ShapeDtypeStruct((b, s * HEADS, HEAD_W), F32),
        jax.ShapeDtypeStruct((b, POOL_PAD, SEG), F32),
        jax.ShapeDtypeStruct((b, s, SEG), BF16),
        jax.ShapeDtypeStruct((b, s, SEG), BF16),
        jax.ShapeDtypeStruct((b, s, SEG), BF16),
        jax.ShapeDtypeStruct((b, s, SEG), BF16),
        jax.ShapeDtypeStruct((b, s, SEG), BF16),
        jax.ShapeDtypeStruct((b, s, SEG), BF16),
        jax.ShapeDtypeStruct((b, s, SEG), BF16),
    )
    out_specs = (row(SEG), pl.BlockSpec((None, tm * HEADS, HEAD_W), lambda bi, i: (bi, i, 0)),
                 pl.BlockSpec((None, POOL_PAD, SEG), lambda bi, i: (bi, 0, 0)),
                 row(SEG), row(SEG), row(SEG), row(SEG), row(SEG), row(SEG), row(SEG))
    return pl.pallas_call(
        _inproj_prompt_kernel,
        out_shape=out_shape,
        grid=(b, s // tm),
        in_specs=[row(D_MODEL), const((1, D_MODEL)), const((D_MODEL, N_SEG * SEG)),
                  const((1, SEG)), const((1, SEG)), const((1, SEG)),
                  const((len(POOL_WINDOWS), HEAD_W, HEAD_W)), const((1, SEG))],
        out_specs=out_specs,
        scratch_shapes=[pltpu.VMEM((tm + 16, SEG), F32)],
        compiler_params=pltpu.CompilerParams(dimension_semantics=("arbitrary", "arbitrary"),
                                             vmem_limit_bytes=VMEM_LIMIT),
        name="inproj_prompt",
    )(x, ng, w_in, gq, gk, gcq, pw, ps)


def _memkv_kernel(mem_ref, mg_ref, w_ref, gck_ref, mk_ref, mv_ref, mkb_ref, mvb_ref):
    h = _rms(mem_ref[...], mg_ref[...]).astype(BF16)
    k = jnp.dot(h, w_ref[:, 0:SEG], preferred_element_type=F32)
    for c in range(HEADS):
        sl = slice(c * HEAD_W, (c + 1) * HEAD_W)
        kn = _norm_block128(k[:, sl], gck_ref[:, sl])
        mk_ref[:, sl] = kn
        mkb_ref[:, sl] = kn.astype(BF16)
    v = jnp.dot(h, w_ref[:, SEG:2 * SEG], preferred_element_type=F32)
    mv_ref[...] = v
    mvb_ref[...] = v.astype(BF16)


def _memkv(mem, mg, w_kv, gck):
    b = mem.shape[0]
    blk = lambda width: pl.BlockSpec((None, N_MEM, width), lambda bi: (bi, 0, 0))
    const = lambda shape: pl.BlockSpec(shape, lambda bi: (0,) * len(shape))
    return pl.pallas_call(
        _memkv_kernel,
        out_shape=(jax.ShapeDtypeStruct((b, N_MEM, SEG), F32), jax.ShapeDtypeStruct((b, N_MEM, SEG), F32),
                   jax.ShapeDtypeStruct((b, N_MEM, SEG), BF16), jax.ShapeDtypeStruct((b, N_MEM, SEG), BF16)),
        grid=(b,),
        in_specs=[blk(D_MODEL), const((1, D_MODEL)), const((D_MODEL, 2 * SEG)), const((1, SEG))],
        out_specs=(blk(SEG), blk(SEG), blk(SEG), blk(SEG)),
        compiler_params=pltpu.CompilerParams(dimension_semantics=("arbitrary",),
                                             vmem_limit_bytes=VMEM_LIMIT),
        name="mem_kv",
    )(mem, mg, w_kv, gck)


def _attn_kernel(pt_ref, rel_ref, x_ref, qb_ref, kb_ref, vb_ref, cqb_ref, sgb_ref, sgc_ref, mixa_ref,
                 mkb_ref, mvb_ref, wout_ref, hg_ref, lq_ref, lk_ref,
                 sin_hbm, memk_hbm, memv_hbm, kt_hbm, v_hbm,
                 o_ref, sout_hbm,
                 bdiag_ref, bsub_ref, qq_ref, m_ref, l_ref, acc_ref, mix_ref,
                 kbuf, vbuf, ring_sem, sin_buf, mk_buf, mv_buf, fetch_sem, stage, out_sem,
                 qbd_ref, sm_ref, sl_ref, sacc_ref, blast_ref, bnew_ref, kpad_ref, vpad_ref,
                 *, n_tile_steps, n_seq, n_mem_seq, n_new):
    t = ATT_TILE
    npg = PAGES_PER_TILE
    r8 = SAMPLE_ROWS
    hr = 2 * r8
    groups_per_seq = pt_ref.shape[1] // npg
    assert groups_per_seq & (groups_per_seq - 1) == 0
    seq_shift = groups_per_seq.bit_length() - 1
    bi = pl.program_id(0)
    qi = pl.program_id(1)
    nqb = kb_ref.shape[0] // t
    first_step = (bi == 0) & (qi == 0)
    last_step = (bi == pl.num_programs(0) - 1) & (qi == pl.num_programs(1) - 1)
    u_base = bi * (nqb * (nqb + 1) // 2) + jnp.right_shift(qi * (qi + 1), 1)

    lane = lax.broadcasted_iota(jnp.int32, (1, SEG), 1)

    def static(v):
        return isinstance(v, int)

    def ring_slot(pos):
        return pos % RING if static(pos) else lax.rem(pos, jnp.int32(RING))

    def clamp(v, hi):
        return min(v, hi) if static(v) else jnp.minimum(v, hi)

    def ring_copies(group, slot):
        out = []
        for i in range(npg):
            if group is None:
                pid = 0
            else:
                pid = pt_ref[jnp.right_shift(group, seq_shift), (group & (groups_per_seq - 1)) * npg + i]
            out.append(pltpu.make_async_copy(kt_hbm.at[pid], kbuf.at[slot, i], ring_sem.at[slot]))
            out.append(pltpu.make_async_copy(v_hbm.at[pid], vbuf.at[slot, i], ring_sem.at[slot]))
        return out

    def ring_start(pos):
        copies = ring_copies(jnp.minimum(pos, n_tile_steps - 1), ring_slot(pos))
        for n, cp in enumerate(copies):
            cp.start(priority=n % 2)

    def ring_wait(pos):
        for cp in ring_copies(None, ring_slot(pos)):
            cp.wait()

    def fetch_copies(seq):
        slot = seq & 1
        s_in = clamp(seq, n_seq - 1)
        s_mem = clamp(seq, n_mem_seq - 1)
        return (pltpu.make_async_copy(sin_hbm.at[s_in], sin_buf.at[slot], fetch_sem.at[slot]),
                pltpu.make_async_copy(memk_hbm.at[s_mem], mk_buf.at[slot], fetch_sem.at[slot]),
                pltpu.make_async_copy(memv_hbm.at[s_mem], mv_buf.at[slot], fetch_sem.at[slot]))

    def out_copy(row, slot):
        return pltpu.make_async_copy(stage.at[slot], sout_hbm.at[row], out_sem.at[slot])

    def sample_begin(slot):
        q8 = sin_buf[slot, 0:r8, :].astype(BF16)
        zero = jnp.zeros_like(q8)
        for hm in range(2 * HEADS):
            qbd_ref[hm * r8:(hm + 1) * r8, :] = jnp.where(
                (lane >= hm * MAP_W) & (lane < (hm + 1) * MAP_W), q8, zero)
        sm_ref[...] = jnp.full(sm_ref.shape, NEG_INF, F32)
        sl_ref[...] = jnp.zeros(sl_ref.shape, F32)
        sacc_ref[...] = jnp.zeros(sacc_ref.shape, F32)

    def sample_attend(s_parts, v_of):
        n = len(s_parts)
        s = jnp.concatenate(s_parts, axis=1) if n > 1 else s_parts[0]
        m_old = sm_ref[...]
        m_new = jnp.maximum(m_old, jnp.max(s, axis=-1, keepdims=True))
        alpha = jnp.exp(m_old - m_new)
        p = jnp.exp(s - (jnp.concatenate([m_new] * n, axis=1) if n > 1 else m_new))
        psum = p[:, 0:PAGE]
        for i in range(1, n):
            psum = psum + p[:, i * PAGE:(i + 1) * PAGE]
        sl_ref[...] = alpha * sl_ref[...] + psum
        pb = p.astype(BF16)
        heads = []
        for h in range(HEADS):
            pv = None
            for i in range(n):
                part = jnp.dot(pb[h * hr:(h + 1) * hr, i * PAGE:(i + 1) * PAGE], v_of(i, h),
                               preferred_element_type=F32)
                pv = part if pv is None else pv + part
            heads.append(pv)
        sacc_ref[...] = alpha * sacc_ref[...] + jnp.concatenate(heads, axis=0)
        sm_ref[...] = m_new

    def sample_scores(u):
        slot = ring_slot(u)
        is_last_group = (u & (groups_per_seq - 1)) == groups_per_seq - 1
        last_bias = jnp.where(is_last_group, blast_ref[...], 0.0)
        qbd = qbd_ref[...]
        s_parts = []
        for i in range(npg):
            s = jnp.dot(qbd, kbuf[slot, i].astype(BF16), preferred_element_type=F32)
            if i == npg - 1:
                s = s + last_bias
            s_parts.append(s)
        return s_parts

    def sample_values(u, s_parts):
        slots = [ring_slot(u + g) for g in range(len(s_parts) // npg)]
        sample_attend(s_parts, lambda i, h: vbuf[slots[i // npg], i % npg,
                                                 pl.ds(h, PAGE, stride=HEADS), :].astype(BF16))

    def sample_end(u):
        seq = jnp.right_shift(u, seq_shift)
        slot = seq & 1
        kpad_ref[0:r8, :] = sin_buf[slot, 4 * r8:5 * r8, :].astype(BF16)
        vpad_ref[0:r8, :] = sin_buf[slot, 5 * r8:6 * r8, :].astype(BF16)
        s_new = lax.dot_general(qbd_ref[...], kpad_ref[...], _NT, preferred_element_type=F32) + bnew_ref[...]
        sample_attend([s_new], lambda i, h: vpad_ref[:, h * HEAD_W:(h + 1) * HEAD_W])

        lam = _lam(lq_ref, lk_ref)
        o = sacc_ref[...] / jnp.sum(sl_ref[...], axis=-1, keepdims=True)
        out_copy(0, slot).wait()
        cq8 = sin_buf[slot, r8:2 * r8, :].astype(BF16)
        for h in range(HEADS):
            sl = slice(h * HEAD_W, (h + 1) * HEAD_W)
            a = o[h * hr:h * hr + r8, :] - lam * o[h * hr + r8:(h + 1) * hr, :]
            od = _norm_block128(a, hg_ref[:, sl]) * (1.0 - LAM_INIT)
            stage[slot, 0:r8, sl] = od * sin_buf[slot, 2 * r8:3 * r8, sl]
            mk_h = mk_buf[slot, pl.ds(h, N_MEM, stride=HEADS), :].astype(BF16)
            mv_h = mv_buf[slot, pl.ds(h, N_MEM, stride=HEADS), :].astype(BF16)
            sc = lax.dot_general(cq8[:, sl], mk_h, _NT, preferred_element_type=F32) * CROSS_SCALE
            pc = jnp.exp(sc - jnp.max(sc, axis=-1, keepdims=True))
            oc = jnp.dot(pc.astype(BF16), mv_h, preferred_element_type=F32)
            oc = oc / jnp.sum(pc, axis=-1, keepdims=True)
            stage[slot, r8:2 * r8, sl] = oc * sin_buf[slot, 3 * r8:4 * r8, sl]
        out_copy(seq, slot).start()

        for cp in fetch_copies(seq + 1):
            cp.wait()
        sample_begin(1 - slot)
        for cp in fetch_copies(seq + 2):
            cp.start()

    @pl.when(first_step)
    def _():
        for g in range(LOOKAHEAD):
            ring_start(g)
        for cp in fetch_copies(0):
            cp.start()
        d = lax.broadcasted_iota(jnp.int32, (t, t), 0) - lax.broadcasted_iota(jnp.int32, (t, t), 1)
        for h in range(HEADS):
            bdiag_ref[h] = jnp.where(d >= 0, _shifted_bias(d, rel_ref, h) * LOG2E, NEG_INF)
            bsub_ref[h] = _shifted_bias(d + t, rel_ref, h) * LOG2E
        tok = lax.broadcasted_iota(jnp.int32, (hr, PAGE), 0) & (r8 - 1)
        key = lax.broadcasted_iota(jnp.int32, (hr, PAGE), 1)
        for h in range(HEADS):
            blast_ref[h * hr:(h + 1) * hr, :] = _shifted_bias(PAGE + tok - key, rel_ref, h)
            ok = (key <= tok) & (key < n_new)
            bnew_ref[h * hr:(h + 1) * hr, :] = jnp.where(ok, _shifted_bias(tok - key, rel_ref, h), NEG_INF)
        kpad_ref[...] = jnp.zeros(kpad_ref.shape, BF16)
        vpad_ref[...] = jnp.zeros(vpad_ref.shape, BF16)
        stage[...] = jnp.zeros(stage.shape, F32)
        out_copy(n_seq, 0).start()
        out_copy(n_seq + 1, 1).start()
        for cp in fetch_copies(0):
            cp.wait()
        sample_begin(0)
        for cp in fetch_copies(1):
            cp.start()

    lam = _lam(lq_ref, lk_ref)
    lo = lax.broadcasted_iota(jnp.int32, (1, HEAD_W), 1) < MAP_W
    mix_ref[:, 0:SEG] = mixa_ref[...]

    for h in range(HEADS):
        qh = qb_ref[:, h * HEAD_W:(h + 1) * HEAD_W]
        zero = jnp.zeros_like(qh)
        qq_ref[h, 0:t, :] = jnp.where(lo, qh, zero)
        qq_ref[h, t:2 * t, :] = jnp.where(lo, zero, qh)
    m_ref[...] = jnp.full(m_ref.shape, NEG_INF, F32)
    l_ref[...] = jnp.zeros(l_ref.shape, F32)
    acc_ref[...] = jnp.zeros(acc_ref.shape, F32)

    def tile_step(j0, bias_ref, u, n):
        w = n * t
        for g in range(n):
            ring_start(u + LOOKAHEAD + g)
        for g in range(n):
            ring_wait(u + g)
        s_parts = []
        for g in range(n):
            s_parts += sample_scores(u + g)
        def head_scores(h):
            s = lax.dot_general(qq_ref[h], kb_ref[pl.ds(j0, w), h * HEAD_W:(h + 1) * HEAD_W], _NT,
                                preferred_element_type=F32)
            if bias_ref is not None:
                bias = bias_ref[h]
                s = s + jnp.concatenate([bias, bias], axis=0)
            return s

        scores = [head_scores(h) for h in range(HEADS)] if n > 1 else None
        for h in range(HEADS):
            sl = slice(h * HEAD_W, (h + 1) * HEAD_W)
            s = scores[h] if n > 1 else head_scores(h)
            m_old = m_ref[h]
            m_new = jnp.maximum(m_old, jnp.max(s, axis=-1, keepdims=True))
            alpha = jnp.exp2(m_old - m_new)
            p = jnp.exp2(s - jnp.concatenate([m_new] * (w // HEAD_W), axis=1))
            psum = p[:, 0:HEAD_W]
            for c in range(1, w // HEAD_W):
                psum = psum + p[:, c * HEAD_W:(c + 1) * HEAD_W]
            l_ref[h] = alpha * l_ref[h] + psum
            acc_ref[h] = alpha * acc_ref[h] + jnp.dot(p.astype(BF16), vb_ref[pl.ds(j0, w), sl],
                                                      preferred_element_type=F32)
            m_ref[h] = m_new
        sample_values(u, s_parts)

        u_last = u + n - 1

        @pl.when((u_last & (groups_per_seq - 1)) == groups_per_seq - 1)
        def _():
            sample_end(u_last)

    tile_step(pl.multiple_of(qi * t, t), bdiag_ref, u_base, 1)

    @pl.when(qi >= 1)
    def _():
        tile_step(pl.multiple_of((qi - 1) * t, t), bsub_ref, u_base + 1, 1)

    n_far = jnp.maximum(qi - 1, 0)
    u_far = u_base + 2
    lead = jnp.where((n_far >= 1) & ((u_far & 1) == 1), 1, 0)
    n_pairs = jnp.right_shift(n_far - lead, 1)
    trail = (n_far - lead) & 1

    @pl.when(lead == 1)
    def _():
        tile_step(0, None, u_far, 1)

    def far_pair(k, carry):
        j = lead + 2 * k
        tile_step(pl.multiple_of(j * t, t), None, u_far + j, 2)
        return carry

    lax.fori_loop(0, n_pairs, far_pair, 0)

    @pl.when(trail == 1)
    def _():
        tile_step(pl.multiple_of((n_far - 1) * t, t), None, u_far + n_far - 1, 1)

    for h in range(HEADS):
        sl = slice(h * HEAD_W, (h + 1) * HEAD_W)
        o = acc_ref[h] / jnp.sum(l_ref[h], axis=-1, keepdims=True)
        a = o[0:t, :] - lam * o[t:2 * t, :]
        od = _norm_block128(a, hg_ref[:, sl]) * (1.0 - LAM_INIT)
        mix_ref[:, SEG + h * HEAD_W:SEG + (h + 1) * HEAD_W] = (od * sgb_ref[:, sl].astype(F32)).astype(BF16)

        sc = lax.dot_general(cqb_ref[:, sl], mkb_ref[:, sl], _NT, preferred_element_type=F32) * CROSS_SCALE
        pc = jnp.exp(sc - jnp.max(sc, axis=-1, keepdims=True))
        c = jnp.dot(pc.astype(BF16), mvb_ref[:, sl], preferred_element_type=F32)
        c = c / jnp.sum(pc, axis=-1, keepdims=True)
        mix_ref[:, 2 * SEG + h * HEAD_W:2 * SEG + (h + 1) * HEAD_W] = (c * sgc_ref[:, sl].astype(F32)).astype(BF16)

    o_ref[...] = x_ref[...] + jnp.dot(mix_ref[...], wout_ref[...], preferred_element_type=F32)

    @pl.when(last_step)
    def _():
        for g in range(LOOKAHEAD):
            ring_wait(n_tile_steps + g)
        for cp in fetch_copies(n_seq + 1):
            cp.wait()
        out_copy(0, 0).wait()
        out_copy(0, 1).wait()


def _attn(page_table, rel_flat, x, qb, kb, vb, cqb, sgb, sgc, mixa, mkb, mvb, w_out, hg, lq, lk,
          sin, mem_k3, mem_v3, cache_kt, cache_v3, n_new):
    b, s, _ = x.shape
    t = ATT_TILE
    nqb = s // t
    n_tile_steps = b * nqb * (nqb + 1) // 2
    n_seq = page_table.shape[0]
    assert n_tile_steps * PAGES_PER_TILE == n_seq * page_table.shape[1]
    assert sin.shape == (n_seq, SIN_ROWS, SEG)
    row = lambda width: pl.BlockSpec((None, t, width), lambda bi, i, pt: (bi, i, 0))
    seq = lambda n: pl.BlockSpec((None, n, SEG), lambda bi, i, pt: (bi, 0, 0), pipeline_mode=pl.Buffered(1))
    const = lambda shape: pl.BlockSpec(shape, lambda bi, i, pt: (0,) * len(shape))
    hbm = pl.BlockSpec(memory_space=pl.ANY)
    r8 = SAMPLE_ROWS
    nq = HEADS * 2 * r8
    grid_spec = pltpu.PrefetchScalarGridSpec(
        num_scalar_prefetch=1,
        grid=(b, nqb),
        in_specs=[pl.BlockSpec(memory_space=pltpu.SMEM),
                  row(D_MODEL), row(SEG), seq(s), seq(s), row(SEG), row(SEG), row(SEG), row(SEG),
                  seq(N_MEM), seq(N_MEM), const((3 * SEG, D_MODEL)), const((1, SEG)),
                  const((2, MAP_W)), const((2, MAP_W)), hbm, hbm, hbm, hbm, hbm],
        out_specs=(row(D_MODEL), hbm),
        scratch_shapes=[
            pltpu.VMEM((HEADS, t, t), F32), pltpu.VMEM((HEADS, t, t), F32),
            pltpu.VMEM((HEADS, 2 * t, HEAD_W), BF16),
            pltpu.VMEM((HEADS, 2 * t, HEAD_W), F32), pltpu.VMEM((HEADS, 2 * t, HEAD_W), F32),
            pltpu.VMEM((HEADS, 2 * t, HEAD_W), F32), pltpu.VMEM((t, 3 * SEG), BF16),
            pltpu.VMEM((RING, PAGES_PER_TILE, SEG, PAGE), F32),
            pltpu.VMEM((RING, PAGES_PER_TILE, PAGE * HEADS, HEAD_W), F32),
            pltpu.SemaphoreType.DMA((RING,)),
            pltpu.VMEM((2, SIN_ROWS, SEG), F32), pltpu.VMEM((2, N_MEM * HEADS, HEAD_W), F32),
            pltpu.VMEM((2, N_MEM * HEADS, HEAD_W), F32), pltpu.SemaphoreType.DMA((2,)),
            pltpu.VMEM((2, 2 * r8, SEG), F32), pltpu.SemaphoreType.DMA((2,)),
            pltpu.VMEM((nq, SEG), BF16),
            pltpu.VMEM((nq, HEAD_W), F32), pltpu.VMEM((nq, HEAD_W), F32), pltpu.VMEM((nq, HEAD_W), F32),
            pltpu.VMEM((nq, PAGE), F32), pltpu.VMEM((nq, PAGE), F32),
            pltpu.VMEM((PAGE, SEG), BF16), pltpu.VMEM((PAGE, SEG), BF16)],
    )
    kern = functools.partial(_attn_kernel, n_tile_steps=n_tile_steps, n_seq=n_seq,
                             n_mem_seq=mem_k3.shape[0], n_new=n_new)
    return pl.pallas_call(
        kern,
        out_shape=(jax.ShapeDtypeStruct((b, s, D_MODEL), F32),
                   jax.ShapeDtypeStruct((n_seq + 2, 2 * r8, SEG), F32)),
        grid_spec=grid_spec,
        compiler_params=pltpu.CompilerParams(dimension_semantics=("arbitrary", "arbitrary"),
                                             vmem_limit_bytes=VMEM_LIMIT_ATTN),
        name="attn",
    )(page_table, rel_flat, x, qb, kb, vb, cqb, sgb, sgc, mixa, mkb, mvb, w_out, hg, lq, lk,
      sin, mem_k3, mem_v3, cache_kt, cache_v3)


def _inproj_sample_kernel(x_ref, st_ref, ng_ref, w_ref, gq_ref, gk_ref, gcq_ref, pw_ref, ps_ref,
                          k_ref, v_ref, pool_ref, q_ref, cq_ref, sgb_ref, sgc_ref, mixa_ref, d_ref):
    nb = st_ref.shape[1]
    nt = x_ref.shape[0] // nb
    h = _rms(x_ref[...], ng_ref[...]).astype(BF16)
    u = jnp.dot(h, w_ref[:, 0:SEG], preferred_element_type=F32)
    sga = _silu(jnp.dot(h, w_ref[:, SEG:2 * SEG], preferred_element_type=F32))

    def ext(e, sl):
        if e < POOL_PAD:
            return st_ref[e, :, sl]
        return u[(e - POOL_PAD) * nb:(e - POOL_PAD + 1) * nb, sl]

    for g, win in enumerate(POOL_WINDOWS):
        sl = slice(g * HEAD_W, (g + 1) * HEAD_W)
        for tk in range(nt):
            acc = ext(POOL_PAD + tk, sl)
            for j in range(1, win):
                acc = acc + ext(POOL_PAD + tk - j, sl)
            d_ref[tk * nb:(tk + 1) * nb, sl] = acc / float(win) - ext(POOL_PAD + tk, sl)
        y = jnp.dot(d_ref[:, sl].astype(BF16), pw_ref[g], preferred_element_type=F32) * ps_ref[:, sl]
        mixa_ref[:, sl] = y * sga[:, sl]

    keep = POOL_PAD - nt
    pool_ref[0:keep] = st_ref[nt:POOL_PAD]
    for tk in range(nt):
        pool_ref[keep + tk] = u[tk * nb:(tk + 1) * nb, :]

    _project_segments(h, w_ref, gq_ref, gk_ref, gcq_ref, k_ref, v_ref, q_ref, None, None,
                      cq_ref, sgb_ref, sgc_ref, q_scale=DIFF_SCALE, v_head_rows=False)


def _inproj_sample(x_tm, state_tm, ng, w_in, gq, gk, gcq, pw, ps):
    rows = x_tm.shape[0]
    nb = state_tm.shape[1]
    vmem = pl.BlockSpec(memory_space=pltpu.VMEM)
    mat = jax.ShapeDtypeStruct((rows, SEG), F32)
    return pl.pallas_call(
        _inproj_sample_kernel,
        out_shape=(mat, mat, jax.ShapeDtypeStruct((POOL_PAD, nb, SEG), F32), mat, mat, mat, mat, mat),
        in_specs=[vmem] * 9,
        out_specs=(vmem,) * 8,
        scratch_shapes=[pltpu.VMEM((rows, SEG), F32)],
        compiler_params=pltpu.CompilerParams(vmem_limit_bytes=VMEM_LIMIT),
        name="inproj_sample",
    )(x_tm, state_tm, ng, w_in, gq, gk, gcq, pw, ps)


def _outproj_kernel(x_ref, a_ref, d_ref, c_ref, w_ref, o_ref):
    y = jnp.dot(a_ref[...].astype(BF16), w_ref[0:SEG, :], preferred_element_type=F32)
    y = y + jnp.dot(d_ref[...].astype(BF16), w_ref[SEG:2 * SEG, :], preferred_element_type=F32)
    y = y + jnp.dot(c_ref[...].astype(BF16), w_ref[2 * SEG:3 * SEG, :], preferred_element_type=F32)
    o_ref[...] = x_ref[...] + y


def _outproj(x, a, d, c, w_out):
    rows = x.shape[0]
    vmem = pl.BlockSpec(memory_space=pltpu.VMEM)
    return pl.pallas_call(
        _outproj_kernel,
        out_shape=jax.ShapeDtypeStruct((rows, D_MODEL), F32),
        in_specs=[vmem] * 5,
        out_specs=vmem,
        compiler_params=pltpu.CompilerParams(vmem_limit_bytes=VMEM_LIMIT),
        name="outproj_sample",
    )(x, a, d, c, w_out)


def _tile_gain(g, reps):
    return jnp.tile(g.reshape(1, -1), (1, reps))


def kernel(x_prompt, x_sample, mem_prompt, cache_k, cache_v, cache_mem_k, cache_mem_v, state_pool, page_table,
           norm_g, w_in, q_norm_g, k_norm_g, cq_norm_g, ck_norm_g, mem_norm_g, w_mem_kv, lam_q, lam_k,
           head_norm_g, pool_w, pool_scale, rel_bias, w_out):
    depth = w_in.shape[0]
    assert depth == 1
    l = 0
    b, s, _ = x_prompt.shape
    nb, nt, _ = x_sample.shape
    r8 = SAMPLE_ROWS

    ng = norm_g[l].reshape(1, D_MODEL)
    w_in_b = w_in[l].astype(BF16)
    gq = _tile_gain(q_norm_g[l], SEG // MAP_W)
    gk = _tile_gain(k_norm_g[l], SEG // MAP_W)
    gcq = _tile_gain(cq_norm_g[l], HEADS)
    gck = _tile_gain(ck_norm_g[l], HEADS)
    pw = pool_w[l].astype(BF16)
    ps = pool_scale[l].reshape(1, SEG)
    hg = head_norm_g[l].reshape(1, SEG)
    w_out_b = w_out[l].astype(BF16)
    rel_flat = rel_bias.reshape(-1)

    k_p, v_p, pool_p, qb, kb, vb, cqb, sgb, sgc, mixa = _inproj_prompt(x_prompt, ng, w_in_b, gq, gk, gcq, pw, ps)
    mk, mv, mkb, mvb = _memkv(mem_prompt, mem_norm_g[l].reshape(1, D_MODEL), w_mem_kv[l].astype(BF16), gck)
    x_tm = x_sample.transpose(1, 0, 2).reshape(nt * nb, D_MODEL)
    st_tm = state_pool[l].transpose(1, 0, 2)
    k_s, v_s, pool_s, q_s, cq_s, sgb_s, sgc_s, mixa_s = _inproj_sample(x_tm, st_tm, ng, w_in_b, gq, gk, gcq, pw, ps)

    def seq_major(a, rows):
        a = a.reshape(nt, nb, SEG).transpose(1, 0, 2)
        return a if rows == nt else jnp.pad(a, ((0, 0), (0, rows - nt), (0, 0)))

    kn = seq_major(k_s, nt)
    vn = seq_major(v_s, nt)

    nqb = s // ATT_TILE
    n_tile_steps = b * nqb * (nqb + 1) // 2
    n_pages = page_table.shape[1]
    n_seq = n_tile_steps * PAGES_PER_TILE // n_pages
    assert n_seq >= nb and n_seq * n_pages == n_tile_steps * PAGES_PER_TILE
    pt_pad = jnp.concatenate([page_table, page_table[:n_seq - nb]], axis=0)
    sin = jnp.concatenate([seq_major(q_s, r8), seq_major(cq_s, r8), seq_major(sgb_s, r8), seq_major(sgc_s, r8),
                           jnp.pad(kn, ((0, 0), (0, r8 - nt), (0, 0))), jnp.pad(vn, ((0, 0), (0, r8 - nt), (0, 0)))],
                          axis=1)
    sin = jnp.pad(sin, ((0, n_seq - nb), (0, 0), (0, 0)))
    n_pool = cache_k.shape[1]
    cache_kt = jnp.transpose(cache_k[l], (0, 2, 3, 4, 1)).reshape(n_pool, SEG, PAGE)
    cache_v3 = cache_v[l].reshape(n_pool, PAGE * HEADS, HEAD_W)
    mem_k3 = cache_mem_k[l].reshape(nb, N_MEM * HEADS, HEAD_W)
    mem_v3 = cache_mem_v[l].reshape(nb, N_MEM * HEADS, HEAD_W)
    y_p, sres = _attn(pt_pad, rel_flat, x_prompt, qb, kb, vb, cqb, sgb, sgc, mixa, mkb, mvb, w_out_b, hg,
                      lam_q[l], lam_k[l], sin, mem_k3, mem_v3, cache_kt, cache_v3, nt)
    mixd = sres[:nb, 0:r8]
    mixc = sres[:nb, r8:2 * r8]

    x8 = jnp.pad(x_sample, ((0, 0), (0, r8 - nt), (0, 0))).reshape(nb * r8, D_MODEL)
    y8 = _outproj(x8, seq_major(mixa_s, r8).reshape(nb * r8, SEG), mixd.reshape(nb * r8, SEG),
                  mixc.reshape(nb * r8, SEG), w_out_b)
    y_s = y8.reshape(nb, r8, D_MODEL)[:, :nt]

    return (y_p, y_s,
            k_p.reshape(1, b, s, HEADS, 2, MAP_W), v_p.reshape(1, b, s, HEADS, HEAD_W),
            pool_p[None], mk.reshape(1, b, N_MEM, HEADS, HEAD_W), mv.reshape(1, b, N_MEM, HEADS, HEAD_W),
            kn.reshape(1, nb, nt, HEADS, 2, MAP_W), vn.reshape(1, nb, nt, HEADS, HEAD_W),
            pool_s.transpose(1, 0, 2)[None])
```

```python
import functools
import math

import jax
import jax.numpy as jnp
from jax import lax
from jax.experimental import pallas as pl
from jax.experimental.pallas import tpu as pltpu

F32 = jnp.float32
BF16 = jnp.bfloat16

D_MODEL = 1024
SEG = 512
N_SEG = 8
POOL_WINDOWS = (2, 4, 8, 16)
POOL_PAD = 15
HEADS = 4
HEAD_W = 128
MAP_W = 64
N_MEM = 256
PAGE = 128
NUM_BUCKETS = 32
EPS = 1e-6
NEG_INF = -1e30
LAM_INIT = 0.8 - 0.6 * math.exp(-0.3 * 0)
DIFF_SCALE = MAP_W ** -0.5
CROSS_SCALE = HEAD_W ** -0.5
LOG2E = math.log2(math.e)

ROW_TILE = 512
ATT_TILE = 256
SAMPLE_ROWS = 8
PAGES_PER_TILE = 8
LOOKAHEAD = 3
RING = LOOKAHEAD + 2
SIN_ROWS = 6 * SAMPLE_ROWS
VMEM_LIMIT = 48 * 1024 * 1024
VMEM_LIMIT_ATTN = 56 * 1024 * 1024

_NT = (((1,), (1,)), ((), ()))


def _rms(x, g):
    ms = jnp.mean(x * x, axis=-1, keepdims=True)
    return x * lax.rsqrt(ms + EPS) * g


def _silu(x):
    return x * jax.nn.sigmoid(x)


def _norm_block64(blk, g):
    lo = lax.broadcasted_iota(jnp.int32, (1, HEAD_W), 1) < MAP_W
    sq = blk * blk
    s_lo = jnp.sum(jnp.where(lo, sq, 0.0), axis=-1, keepdims=True)
    s_hi = jnp.sum(jnp.where(lo, 0.0, sq), axis=-1, keepdims=True)
    r = jnp.where(lo, lax.rsqrt(s_lo * (1.0 / MAP_W) + EPS), lax.rsqrt(s_hi * (1.0 / MAP_W) + EPS))
    return blk * r * g


def _norm_block128(blk, g):
    ms = jnp.mean(blk * blk, axis=-1, keepdims=True)
    return blk * lax.rsqrt(ms + EPS) * g


def _lam(lq_ref, lk_ref):
    e = jnp.exp(jnp.sum(lq_ref[...] * lk_ref[...], axis=-1, keepdims=True))
    return e[0:1, :] - e[1:2, :] + LAM_INIT


def _shifted_bias(dist, rel_ref, h):
    n = jnp.maximum(dist, 0)
    nf = jnp.maximum(n, 1).astype(F32)
    large = 16 + (jnp.log(nf / 16) / math.log(128 / 16) * 16).astype(jnp.int32)
    large = jnp.minimum(large, NUM_BUCKETS - 1)
    bucket = jnp.where(n < 16, n, large)
    far = rel_ref[(NUM_BUCKETS - 1) * HEADS + h]
    out = jnp.zeros(dist.shape, F32)
    for b in range(NUM_BUCKETS - 1):
        out = jnp.where(bucket == b, rel_ref[b * HEADS + h] - far, out)
    return out


def _project_segments(h, w_ref, gq_ref, gk_ref, gcq_ref, k_ref, v_ref, qb_ref, kb_ref, vb_ref,
                      cqb_ref, sgb_ref, sgc_ref, q_scale, v_head_rows):
    rows = h.shape[0]

    def seg(s):
        return jnp.dot(h, w_ref[:, s * SEG:(s + 1) * SEG], preferred_element_type=F32)

    q = seg(2)
    for c in range(HEADS):
        sl = slice(c * HEAD_W, (c + 1) * HEAD_W)
        qb_ref[:, sl] = (_norm_block64(q[:, sl], gq_ref[:, sl]) * q_scale).astype(qb_ref.dtype)
    k = seg(3)
    for c in range(HEADS):
        sl = slice(c * HEAD_W, (c + 1) * HEAD_W)
        kn = _norm_block64(k[:, sl], gk_ref[:, sl])
        k_ref[:, sl] = kn
        if kb_ref is not None:
            kb_ref[:, sl] = kn.astype(BF16)
    v = seg(4)
    if v_head_rows:
        for c in range(HEADS):
            v_ref[pl.ds(c, rows, stride=HEADS), :] = v[:, c * HEAD_W:(c + 1) * HEAD_W]
    else:
        v_ref[...] = v
    if vb_ref is not None:
        vb_ref[...] = v.astype(BF16)
    sgb_ref[...] = _silu(seg(5)).astype(sgb_ref.dtype)
    cq = seg(6)
    for c in range(HEADS):
        sl = slice(c * HEAD_W, (c + 1) * HEAD_W)
        cqb_ref[:, sl] = _norm_block128(cq[:, sl], gcq_ref[:, sl]).astype(cqb_ref.dtype)
    sgc_ref[...] = _silu(seg(7)).astype(sgc_ref.dtype)


def _inproj_prompt_kernel(x_ref, ng_ref, w_ref, gq_ref, gk_ref, gcq_ref, pw_ref, ps_ref,
                          k_ref, v_ref, pool_ref, qb_ref, kb_ref, vb_ref, cqb_ref, sgb_ref, sgc_ref,
                          mixa_ref, uext_ref):
    tm = ROW_TILE
    i = pl.program_id(1)

    @pl.when(i == 0)
    def _():
        uext_ref[0:16, :] = jnp.zeros((16, SEG), F32)

    h = _rms(x_ref[...], ng_ref[...]).astype(BF16)
    u = jnp.dot(h, w_ref[:, 0:SEG], preferred_element_type=F32)
    uext_ref[16:16 + tm, :] = u
    sga = _silu(jnp.dot(h, w_ref[:, SEG:2 * SEG], preferred_element_type=F32))
    pos = i * tm + lax.broadcasted_iota(jnp.int32, (tm, 1), 0)
    for g, win in enumerate(POOL_WINDOWS):
        sl = slice(g * HEAD_W, (g + 1) * HEAD_W)
        ug = u[:, sl]
        acc = ug
        for j in range(1, win):
            acc = acc + uext_ref[16 - j:16 - j + tm, sl]
        cnt = jnp.minimum(pos + 1, win).astype(F32)
        d = acc / cnt - ug
        y = jnp.dot(d.astype(BF16), pw_ref[g], preferred_element_type=F32) * ps_ref[:, sl]
        mixa_ref[:, sl] = (y * sga[:, sl]).astype(BF16)

    @pl.when(i == pl.num_programs(1) - 1)
    def _():
        pool_ref[...] = uext_ref[tm + 1:tm + 16, :]

    uext_ref[0:16, :] = uext_ref[tm:tm + 16, :]

    _project_segments(h, w_ref, gq_ref, gk_ref, gcq_ref, k_ref, v_ref, qb_ref, kb_ref, vb_ref,
                      cqb_ref, sgb_ref, sgc_ref, q_scale=DIFF_SCALE * LOG2E, v_head_rows=True)


def _inproj_prompt(x, ng, w_in, gq, gk, gcq, pw, ps):
    b, s, _ = x.shape
    tm = ROW_TILE
    row = lambda width: pl.BlockSpec((None, tm, width), lambda bi, i: (bi, i, 0))
    const = lambda shape: pl.BlockSpec(shape, lambda bi, i: (0,) * len(shape))
    out_shape = (
        jax.ShapeDtypeStruct((b, s, SEG), F32),
        jax.ShapeDtypeStruct((b, s * HEADS, HEAD_W), F32),
        jax.ShapeDtypeStruct((b, POOL_PAD, SEG), F32),
        jax.ShapeDtypeStruct((b, s, SEG), BF16),
        jax.ShapeDtypeStruct((b, s, SEG), BF16),
        jax.ShapeDtypeStruct((b, s, SEG), BF16),
        jax.ShapeDtypeStruct((b, s, SEG), BF16),
        jax.ShapeDtypeStruct((b, s, SEG), BF16),
        jax.ShapeDtypeStruct((b, s, SEG), BF16),
        jax.ShapeDtypeStruct((b, s, SEG), BF16),
    )
    out_specs = (row(SEG), pl.BlockSpec((None, tm * HEADS, HEAD_W), lambda bi, i: (bi, i, 0)),
                 pl.BlockSpec((None, POOL_PAD, SEG), lambda bi, i: (bi, 0, 0)),
                 row(SEG), row(SEG), row(SEG), row(SEG), row(SEG), row(SEG), row(SEG))
    return pl.pallas_call(
        _inproj_prompt_kernel,
        out_shape=out_shape,
        grid=(b, s // tm),
        in_specs=[row(D_MODEL), const((1, D_MODEL)), const((D_MODEL, N_SEG * SEG)),
                  const((1, SEG)), const((1, SEG)), const((1, SEG)),
                  const((len(POOL_WINDOWS), HEAD_W, HEAD_W)), const((1, SEG))],
        out_specs=out_specs,
        scratch_shapes=[pltpu.VMEM((tm + 16, SEG), F32)],
        compiler_params=pltpu.CompilerParams(dimension_semantics=("arbitrary", "arbitrary"),
                                             vmem_limit_bytes=VMEM_LIMIT),
        name="inproj_prompt",
    )(x, ng, w_in, gq, gk, gcq, pw, ps)


def _memkv_kernel(mem_ref, mg_ref, w_ref, gck_ref, mk_ref, mv_ref, mkb_ref, mvb_ref):
    h = _rms(mem_ref[...], mg_ref[...]).astype(BF16)
    k = jnp.dot(h, w_ref[:, 0:SEG], preferred_element_type=F32)
    for c in range(HEADS):
        sl = slice(c * HEAD_W, (c + 1) * HEAD_W)
        kn = _norm_block128(k[:, sl], gck_ref[:, sl])
        mk_ref[:, sl] = kn
        mkb_ref[:, sl] = kn.astype(BF16)
    v = jnp.dot(h, w_ref[:, SEG:2 * SEG], preferred_element_type=F32)
    mv_ref[...] = v
    mvb_ref[...] = v.astype(BF16)


def _memkv(mem, mg, w_kv, gck):
    b = mem.shape[0]
    blk = lambda width: pl.BlockSpec((None, N_MEM, width), lambda bi: (bi, 0, 0))
    const = lambda shape: pl.BlockSpec(shape, lambda bi: (0,) * len(shape))
    return pl.pallas_call(
        _memkv_kernel,
        out_shape=(jax.ShapeDtypeStruct((b, N_MEM, SEG), F32), jax.ShapeDtypeStruct((b, N_MEM, SEG), F32),
                   jax.ShapeDtypeStruct((b, N_MEM, SEG), BF16), jax.ShapeDtypeStruct((b, N_MEM, SEG), BF16)),
        grid=(b,),
        in_specs=[blk(D_MODEL), const((1, D_MODEL)), const((D_MODEL, 2 * SEG)), const((1, SEG))],
        out_specs=(blk(SEG), blk(SEG), blk(SEG), blk(SEG)),
        compiler_params=pltpu.CompilerParams(dimension_semantics=("arbitrary",),
                                             vmem_limit_bytes=VMEM_LIMIT),
        name="mem_kv",
    )(mem, mg, w_kv, gck)


def _attn_kernel(pt_ref, rel_ref, x_ref, qb_ref, kb_ref, vb_ref, cqb_ref, sgb_ref, sgc_ref, mixa_ref,
                 mkb_ref, mvb_ref, wout_ref, hg_ref, lq_ref, lk_ref,
                 sin_hbm, memk_hbm, memv_hbm, kt_hbm, v_hbm,
                 o_ref, sout_hbm,
                 bdiag_ref, bsub_ref, qq_ref, m_ref, l_ref, acc_ref, mix_ref,
                 kbuf, vbuf, ring_sem, sin_buf, mk_buf, mv_buf, fetch_sem, stage, out_sem,
                 qbd_ref, sm_ref, sl_ref, sacc_ref, blast_ref, bnew_ref, kpad_ref, vpad_ref,
                 *, n_seq, n_mem_seq, n_new):
    t = ATT_TILE
    npg = PAGES_PER_TILE
    r8 = SAMPLE_ROWS
    hr = 2 * r8
    groups_per_seq = pt_ref.shape[1] // npg
    assert groups_per_seq & (groups_per_seq - 1) == 0
    seq_shift = groups_per_seq.bit_length() - 1
    bi = pl.program_id(0)
    qi = pl.program_id(1)
    nqb = kb_ref.shape[0] // t
    first_step = (bi == 0) & (qi == 0)
    last_step = (bi == pl.num_programs(0) - 1) & (qi == pl.num_programs(1) - 1)
    u_base = bi * (nqb * (nqb + 1) // 2) + jnp.right_shift(qi * (qi + 1), 1)

    lane = lax.broadcasted_iota(jnp.int32, (1, SEG), 1)

    def static(v):
        return isinstance(v, int)

    def ring_slot(pos):
        return pos % RING if static(pos) else lax.rem(pos, jnp.int32(RING))

    def clamp(v, hi):
        return min(v, hi) if static(v) else jnp.minimum(v, hi)

    def ring_copies(group, slot):
        out = []
        for i in range(npg):
            if group is None:
                pid = 0
            else:
                pid = pt_ref[jnp.right_shift(group, seq_shift), (group & (groups_per_seq - 1)) * npg + i]
            out.append(pltpu.make_async_copy(kt_hbm.at[pid], kbuf.at[slot, i], ring_sem.at[slot]))
            out.append(pltpu.make_async_copy(v_hbm.at[pid], vbuf.at[slot, i], ring_sem.at[slot]))
        return out

    n_real_groups = n_mem_seq * groups_per_seq

    def ring_start(pos):
        def go():
            for n, cp in enumerate(ring_copies(pos, ring_slot(pos))):
                cp.start(priority=n % 2)
        if static(pos):
            assert pos < n_real_groups
            go()
        else:
            pl.when(pos < n_real_groups)(go)

    def ring_wait(pos):
        def go():
            for cp in ring_copies(None, ring_slot(pos)):
                cp.wait()
        pl.when(pos < n_real_groups)(go)

    def fetch_copies(seq):
        slot = seq & 1
        s_in = clamp(seq, n_seq - 1)
        s_mem = clamp(seq, n_mem_seq - 1)
        return (pltpu.make_async_copy(sin_hbm.at[s_in], sin_buf.at[slot], fetch_sem.at[slot]),
                pltpu.make_async_copy(memk_hbm.at[s_mem], mk_buf.at[slot], fetch_sem.at[slot]),
                pltpu.make_async_copy(memv_hbm.at[s_mem], mv_buf.at[slot], fetch_sem.at[slot]))

    def out_copy(row, slot):
        return pltpu.make_async_copy(stage.at[slot], sout_hbm.at[row], out_sem.at[slot])

    def sample_begin(slot):
        q8 = sin_buf[slot, 0:r8, :].astype(BF16)
        zero = jnp.zeros_like(q8)
        for hm in range(2 * HEADS):
            qbd_ref[hm * r8:(hm + 1) * r8, :] = jnp.where(
                (lane >= hm * MAP_W) & (lane < (hm + 1) * MAP_W), q8, zero)
        sm_ref[...] = jnp.full(sm_ref.shape, NEG_INF, F32)
        sl_ref[...] = jnp.zeros(sl_ref.shape, F32)
        sacc_ref[...] = jnp.zeros(sacc_ref.shape, F32)

    def sample_attend(s_parts, v_of):
        n = len(s_parts)
        s = jnp.concatenate(s_parts, axis=1) if n > 1 else s_parts[0]
        m_old = sm_ref[...]
        m_new = jnp.maximum(m_old, jnp.max(s, axis=-1, keepdims=True))
        alpha = jnp.exp(m_old - m_new)
        p = jnp.exp(s - (jnp.concatenate([m_new] * n, axis=1) if n > 1 else m_new))
        psum = p[:, 0:PAGE]
        for i in range(1, n):
            psum = psum + p[:, i * PAGE:(i + 1) * PAGE]
        sl_ref[...] = alpha * sl_ref[...] + psum
        pb = p.astype(BF16)
        heads = []
        for h in range(HEADS):
            pv = None
            for i in range(n):
                part = jnp.dot(pb[h * hr:(h + 1) * hr, i * PAGE:(i + 1) * PAGE], v_of(i, h),
                               preferred_element_type=F32)
                pv = part if pv is None else pv + part
            heads.append(pv)
        sacc_ref[...] = alpha * sacc_ref[...] + jnp.concatenate(heads, axis=0)
        sm_ref[...] = m_new

    def sample_scores(u):
        slot = ring_slot(u)
        is_last_group = (u & (groups_per_seq - 1)) == groups_per_seq - 1
        last_bias = jnp.where(is_last_group, blast_ref[...], 0.0)
        qbd = qbd_ref[...]
        s_parts = []
        for i in range(npg):
            s = jnp.dot(qbd, kbuf[slot, i].astype(BF16), preferred_element_type=F32)
            if i == npg - 1:
                s = s + last_bias
            s_parts.append(s)
        return s_parts

    def sample_values(u, s_parts):
        slots = [ring_slot(u + g) for g in range(len(s_parts) // npg)]
        sample_attend(s_parts, lambda i, h: vbuf[slots[i // npg], i % npg,
                                                 pl.ds(h, PAGE, stride=HEADS), :].astype(BF16))

    def sample_end(u):
        seq = jnp.right_shift(u, seq_shift)
        slot = seq & 1
        kpad_ref[0:r8, :] = sin_buf[slot, 4 * r8:5 * r8, :].astype(BF16)
        vpad_ref[0:r8, :] = sin_buf[slot, 5 * r8:6 * r8, :].astype(BF16)
        s_new = lax.dot_general(qbd_ref[...], kpad_ref[...], _NT, preferred_element_type=F32) + bnew_ref[...]
        sample_attend([s_new], lambda i, h: vpad_ref[:, h * HEAD_W:(h + 1) * HEAD_W])

        lam = _lam(lq_ref, lk_ref)
        o = sacc_ref[...] / jnp.sum(sl_ref[...], axis=-1, keepdims=True)
        out_copy(0, slot).wait()
        cq8 = sin_buf[slot, r8:2 * r8, :].astype(BF16)
        for h in range(HEADS):
            sl = slice(h * HEAD_W, (h + 1) * HEAD_W)
            a = o[h * hr:h * hr + r8, :] - lam * o[h * hr + r8:(h + 1) * hr, :]
            od = _norm_block128(a, hg_ref[:, sl]) * (1.0 - LAM_INIT)
            stage[slot, 0:r8, sl] = od * sin_buf[slot, 2 * r8:3 * r8, sl]
            mk_h = mk_buf[slot, pl.ds(h, N_MEM, stride=HEADS), :].astype(BF16)
            mv_h = mv_buf[slot, pl.ds(h, N_MEM, stride=HEADS), :].astype(BF16)
            sc = lax.dot_general(cq8[:, sl], mk_h, _NT, preferred_element_type=F32) * CROSS_SCALE
            pc = jnp.exp(sc - jnp.max(sc, axis=-1, keepdims=True))
            oc = jnp.dot(pc.astype(BF16), mv_h, preferred_element_type=F32)
            oc = oc / jnp.sum(pc, axis=-1, keepdims=True)
            stage[slot, r8:2 * r8, sl] = oc * sin_buf[slot, 3 * r8:4 * r8, sl]
        out_copy(seq, slot).start()

        for cp in fetch_copies(seq + 1):
            cp.wait()
        sample_begin(1 - slot)
        for cp in fetch_copies(seq + 2):
            cp.start()

    @pl.when(first_step)
    def _():
        for g in range(LOOKAHEAD):
            ring_start(g)
        for cp in fetch_copies(0):
            cp.start()
        d = lax.broadcasted_iota(jnp.int32, (t, t), 0) - lax.broadcasted_iota(jnp.int32, (t, t), 1)
        for h in range(HEADS):
            bdiag_ref[h] = jnp.where(d >= 0, _shifted_bias(d, rel_ref, h) * LOG2E, NEG_INF)
            bsub_ref[h] = _shifted_bias(d + t, rel_ref, h) * LOG2E
        tok = lax.broadcasted_iota(jnp.int32, (hr, PAGE), 0) & (r8 - 1)
        key = lax.broadcasted_iota(jnp.int32, (hr, PAGE), 1)
        for h in range(HEADS):
            blast_ref[h * hr:(h + 1) * hr, :] = _shifted_bias(PAGE + tok - key, rel_ref, h)
            ok = (key <= tok) & (key < n_new)
            bnew_ref[h * hr:(h + 1) * hr, :] = jnp.where(ok, _shifted_bias(tok - key, rel_ref, h), NEG_INF)
        kpad_ref[...] = jnp.zeros(kpad_ref.shape, BF16)
        vpad_ref[...] = jnp.zeros(vpad_ref.shape, BF16)
        stage[...] = jnp.zeros(stage.shape, F32)
        out_copy(n_seq, 0).start()
        out_copy(n_seq + 1, 1).start()
        for cp in fetch_copies(0):
            cp.wait()
        sample_begin(0)
        for cp in fetch_copies(1):
            cp.start()

    lam = _lam(lq_ref, lk_ref)
    lo = lax.broadcasted_iota(jnp.int32, (1, HEAD_W), 1) < MAP_W
    mix_ref[:, 0:SEG] = mixa_ref[...]

    for h in range(HEADS):
        qh = qb_ref[:, h * HEAD_W:(h + 1) * HEAD_W]
        zero = jnp.zeros_like(qh)
        qq_ref[h, 0:t, :] = jnp.where(lo, qh, zero)
        qq_ref[h, t:2 * t, :] = jnp.where(lo, zero, qh)
    m_ref[...] = jnp.full(m_ref.shape, NEG_INF, F32)
    l_ref[...] = jnp.zeros(l_ref.shape, F32)
    acc_ref[...] = jnp.zeros(acc_ref.shape, F32)

    def tile_step(j0, bias_ref, u, n):
        w = n * t
        for g in range(n):
            ring_start(u + LOOKAHEAD + g)
        for g in range(n):
            ring_wait(u + g)
        s_parts = []
        for g in range(n):
            s_parts += sample_scores(u + g)
        def head_scores(h):
            s = lax.dot_general(qq_ref[h], kb_ref[pl.ds(j0, w), h * HEAD_W:(h + 1) * HEAD_W], _NT,
                                preferred_element_type=F32)
            if bias_ref is not None:
                bias = bias_ref[h]
                s = s + jnp.concatenate([bias, bias], axis=0)
            return s

        scores = [head_scores(h) for h in range(HEADS)] if n > 1 else None
        for h in range(HEADS):
            sl = slice(h * HEAD_W, (h + 1) * HEAD_W)
            s = scores[h] if n > 1 else head_scores(h)
            m_old = m_ref[h]
            m_new = jnp.maximum(m_old, jnp.max(s, axis=-1, keepdims=True))
            alpha = jnp.exp2(m_old - m_new)
            p = jnp.exp2(s - jnp.concatenate([m_new] * (w // HEAD_W), axis=1))
            psum = p[:, 0:HEAD_W]
            for c in range(1, w // HEAD_W):
                psum = psum + p[:, c * HEAD_W:(c + 1) * HEAD_W]
            l_ref[h] = alpha * l_ref[h] + psum
            acc_ref[h] = alpha * acc_ref[h] + jnp.dot(p.astype(BF16), vb_ref[pl.ds(j0, w), sl],
                                                      preferred_element_type=F32)
            m_ref[h] = m_new
        sample_values(u, s_parts)

        u_last = u + n - 1

        @pl.when((u_last & (groups_per_seq - 1)) == groups_per_seq - 1)
        def _():
            sample_end(u_last)

    tile_step(pl.multiple_of(qi * t, t), bdiag_ref, u_base, 1)

    @pl.when(qi >= 1)
    def _():
        tile_step(pl.multiple_of((qi - 1) * t, t), bsub_ref, u_base + 1, 1)

    n_far = jnp.maximum(qi - 1, 0)
    u_far = u_base + 2
    lead = jnp.where((n_far >= 1) & ((u_far & 1) == 1), 1, 0)
    n_pairs = jnp.right_shift(n_far - lead, 1)
    trail = (n_far - lead) & 1

    @pl.when(lead == 1)
    def _():
        tile_step(0, None, u_far, 1)

    def far_pair(k, carry):
        j = lead + 2 * k
        tile_step(pl.multiple_of(j * t, t), None, u_far + j, 2)
        return carry

    lax.fori_loop(0, n_pairs, far_pair, 0)

    @pl.when(trail == 1)
    def _():
        tile_step(pl.multiple_of((n_far - 1) * t, t), None, u_far + n_far - 1, 1)

    for h in range(HEADS):
        sl = slice(h * HEAD_W, (h + 1) * HEAD_W)
        o = acc_ref[h] / jnp.sum(l_ref[h], axis=-1, keepdims=True)
        a = o[0:t, :] - lam * o[t:2 * t, :]
        od = _norm_block128(a, hg_ref[:, sl]) * (1.0 - LAM_INIT)
        mix_ref[:, SEG + h * HEAD_W:SEG + (h + 1) * HEAD_W] = (od * sgb_ref[:, sl].astype(F32)).astype(BF16)

        sc = lax.dot_general(cqb_ref[:, sl], mkb_ref[:, sl], _NT, preferred_element_type=F32) * CROSS_SCALE
        pc = jnp.exp(sc - jnp.max(sc, axis=-1, keepdims=True))
        c = jnp.dot(pc.astype(BF16), mvb_ref[:, sl], preferred_element_type=F32)
        c = c / jnp.sum(pc, axis=-1, keepdims=True)
        mix_ref[:, 2 * SEG + h * HEAD_W:2 * SEG + (h + 1) * HEAD_W] = (c * sgc_ref[:, sl].astype(F32)).astype(BF16)

    o_ref[...] = x_ref[...] + jnp.dot(mix_ref[...], wout_ref[...], preferred_element_type=F32)

    @pl.when(last_step)
    def _():
        for cp in fetch_copies(n_seq + 1):
            cp.wait()
        out_copy(0, 0).wait()
        out_copy(0, 1).wait()


def _attn(page_table, rel_flat, x, qb, kb, vb, cqb, sgb, sgc, mixa, mkb, mvb, w_out, hg, lq, lk,
          sin, mem_k3, mem_v3, cache_kt, cache_v3, n_new):
    b, s, _ = x.shape
    t = ATT_TILE
    nqb = s // t
    n_tile_steps = b * nqb * (nqb + 1) // 2
    n_seq = sin.shape[0]
    assert n_tile_steps * PAGES_PER_TILE == n_seq * page_table.shape[1]
    assert sin.shape == (n_seq, SIN_ROWS, SEG) and page_table.shape[0] == mem_k3.shape[0]
    row = lambda width: pl.BlockSpec((None, t, width), lambda bi, i, pt: (bi, i, 0))
    seq = lambda n: pl.BlockSpec((None, n, SEG), lambda bi, i, pt: (bi, 0, 0), pipeline_mode=pl.Buffered(1))
    const = lambda shape: pl.BlockSpec(shape, lambda bi, i, pt: (0,) * len(shape))
    hbm = pl.BlockSpec(memory_space=pl.ANY)
    r8 = SAMPLE_ROWS
    nq = HEADS * 2 * r8
    grid_spec = pltpu.PrefetchScalarGridSpec(
        num_scalar_prefetch=1,
        grid=(b, nqb),
        in_specs=[pl.BlockSpec(memory_space=pltpu.SMEM),
                  row(D_MODEL), row(SEG), seq(s), seq(s), row(SEG), row(SEG), row(SEG), row(SEG),
                  seq(N_MEM), seq(N_MEM), const((3 * SEG, D_MODEL)), const((1, SEG)),
                  const((2, MAP_W)), const((2, MAP_W)), hbm, hbm, hbm, hbm, hbm],
        out_specs=(row(D_MODEL), hbm),
        scratch_shapes=[
            pltpu.VMEM((HEADS, t, t), F32), pltpu.VMEM((HEADS, t, t), F32),
            pltpu.VMEM((HEADS, 2 * t, HEAD_W), BF16),
            pltpu.VMEM((HEADS, 2 * t, HEAD_W), F32), pltpu.VMEM((HEADS, 2 * t, HEAD_W), F32),
            pltpu.VMEM((HEADS, 2 * t, HEAD_W), F32), pltpu.VMEM((t, 3 * SEG), BF16),
            pltpu.VMEM((RING, PAGES_PER_TILE, SEG, PAGE), F32),
            pltpu.VMEM((RING, PAGES_PER_TILE, PAGE * HEADS, HEAD_W), F32),
            pltpu.SemaphoreType.DMA((RING,)),
            pltpu.VMEM((2, SIN_ROWS, SEG), F32), pltpu.VMEM((2, N_MEM * HEADS, HEAD_W), F32),
            pltpu.VMEM((2, N_MEM * HEADS, HEAD_W), F32), pltpu.SemaphoreType.DMA((2,)),
            pltpu.VMEM((2, 2 * r8, SEG), F32), pltpu.SemaphoreType.DMA((2,)),
            pltpu.VMEM((nq, SEG), BF16),
            pltpu.VMEM((nq, HEAD_W), F32), pltpu.VMEM((nq, HEAD_W), F32), pltpu.VMEM((nq, HEAD_W), F32),
            pltpu.VMEM((nq, PAGE), F32), pltpu.VMEM((nq, PAGE), F32),
            pltpu.VMEM((PAGE, SEG), BF16), pltpu.VMEM((PAGE, SEG), BF16)],
    )
    kern = functools.partial(_attn_kernel, n_seq=n_seq, n_mem_seq=mem_k3.shape[0], n_new=n_new)
    return pl.pallas_call(
        kern,
        out_shape=(jax.ShapeDtypeStruct((b, s, D_MODEL), F32),
                   jax.ShapeDtypeStruct((n_seq + 2, 2 * r8, SEG), F32)),
        grid_spec=grid_spec,
        compiler_params=pltpu.CompilerParams(dimension_semantics=("arbitrary", "arbitrary"),
                                             vmem_limit_bytes=VMEM_LIMIT_ATTN),
        name="attn",
    )(page_table, rel_flat, x, qb, kb, vb, cqb, sgb, sgc, mixa, mkb, mvb, w_out, hg, lq, lk,
      sin, mem_k3, mem_v3, cache_kt, cache_v3)


def _inproj_sample_kernel(x_ref, st_ref, ng_ref, w_ref, gq_ref, gk_ref, gcq_ref, pw_ref, ps_ref,
                          k_ref, v_ref, pool_ref, q_ref, cq_ref, sgb_ref, sgc_ref, mixa_ref, d_ref):
    nb = st_ref.shape[1]
    nt = x_ref.shape[0] // nb
    h = _rms(x_ref[...], ng_ref[...]).astype(BF16)
    u = jnp.dot(h, w_ref[:, 0:SEG], preferred_element_type=F32)
    sga = _silu(jnp.dot(h, w_ref[:, SEG:2 * SEG], preferred_element_type=F32))

    def ext(e, sl):
        if e < POOL_PAD:
            return st_ref[e, :, sl]
        return u[(e - POOL_PAD) * nb:(e - POOL_PAD + 1) * nb, sl]

    for g, win in enumerate(POOL_WINDOWS):
        sl = slice(g * HEAD_W, (g + 1) * HEAD_W)
        for tk in range(nt):
            acc = ext(POOL_PAD + tk, sl)
            for j in range(1, win):
                acc = acc + ext(POOL_PAD + tk - j, sl)
            d_ref[tk * nb:(tk + 1) * nb, sl] = acc / float(win) - ext(POOL_PAD + tk, sl)
        y = jnp.dot(d_ref[:, sl].astype(BF16), pw_ref[g], preferred_element_type=F32) * ps_ref[:, sl]
        mixa_ref[:, sl] = y * sga[:, sl]

    keep = POOL_PAD - nt
    pool_ref[0:keep] = st_ref[nt:POOL_PAD]
    for tk in range(nt):
        pool_ref[keep + tk] = u[tk * nb:(tk + 1) * nb, :]

    _project_segments(h, w_ref, gq_ref, gk_ref, gcq_ref, k_ref, v_ref, q_ref, None, None,
                      cq_ref, sgb_ref, sgc_ref, q_scale=DIFF_SCALE, v_head_rows=False)


def _inproj_sample(x_tm, state_tm, ng, w_in, gq, gk, gcq, pw, ps):
    rows = x_tm.shape[0]
    nb = state_tm.shape[1]
    vmem = pl.BlockSpec(memory_space=pltpu.VMEM)
    mat = jax.ShapeDtypeStruct((rows, SEG), F32)
    return pl.pallas_call(
        _inproj_sample_kernel,
        out_shape=(mat, mat, jax.ShapeDtypeStruct((POOL_PAD, nb, SEG), F32), mat, mat, mat, mat, mat),
        in_specs=[vmem] * 9,
        out_specs=(vmem,) * 8,
        scratch_shapes=[pltpu.VMEM((rows, SEG), F32)],
        compiler_params=pltpu.CompilerParams(vmem_limit_bytes=VMEM_LIMIT),
        name="inproj_sample",
    )(x_tm, state_tm, ng, w_in, gq, gk, gcq, pw, ps)


def _outproj_kernel(x_ref, a_ref, d_ref, c_ref, w_ref, o_ref):
    y = jnp.dot(a_ref[...].astype(BF16), w_ref[0:SEG, :], preferred_element_type=F32)
    y = y + jnp.dot(d_ref[...].astype(BF16), w_ref[SEG:2 * SEG, :], preferred_element_type=F32)
    y = y + jnp.dot(c_ref[...].astype(BF16), w_ref[2 * SEG:3 * SEG, :], preferred_element_type=F32)
    o_ref[...] = x_ref[...] + y


def _outproj(x, a, d, c, w_out):
    rows = x.shape[0]
    vmem = pl.BlockSpec(memory_space=pltpu.VMEM)
    return pl.pallas_call(
        _outproj_kernel,
        out_shape=jax.ShapeDtypeStruct((rows, D_MODEL), F32),
        in_specs=[vmem] * 5,
        out_specs=vmem,
        compiler_params=pltpu.CompilerParams(vmem_limit_bytes=VMEM_LIMIT),
        name="outproj_sample",
    )(x, a, d, c, w_out)


def _tile_gain(g, reps):
    return jnp.tile(g.reshape(1, -1), (1, reps))


def kernel(x_prompt, x_sample, mem_prompt, cache_k, cache_v, cache_mem_k, cache_mem_v, state_pool, page_table,
           norm_g, w_in, q_norm_g, k_norm_g, cq_norm_g, ck_norm_g, mem_norm_g, w_mem_kv, lam_q, lam_k,
           head_norm_g, pool_w, pool_scale, rel_bias, w_out):
    depth = w_in.shape[0]
    assert depth == 1
    l = 0
    b, s, _ = x_prompt.shape
    nb, nt, _ = x_sample.shape
    r8 = SAMPLE_ROWS

    ng = norm_g[l].reshape(1, D_MODEL)
    w_in_b = w_in[l].astype(BF16)
    gq = _tile_gain(q_norm_g[l], SEG // MAP_W)
    gk = _tile_gain(k_norm_g[l], SEG // MAP_W)
    gcq = _tile_gain(cq_norm_g[l], HEADS)
    gck = _tile_gain(ck_norm_g[l], HEADS)
    pw = pool_w[l].astype(BF16)
    ps = pool_scale[l].reshape(1, SEG)
    hg = head_norm_g[l].reshape(1, SEG)
    w_out_b = w_out[l].astype(BF16)
    rel_flat = rel_bias.reshape(-1)

    k_p, v_p, pool_p, qb, kb, vb, cqb, sgb, sgc, mixa = _inproj_prompt(x_prompt, ng, w_in_b, gq, gk, gcq, pw, ps)
    mk, mv, mkb, mvb = _memkv(mem_prompt, mem_norm_g[l].reshape(1, D_MODEL), w_mem_kv[l].astype(BF16), gck)
    x_tm = x_sample.transpose(1, 0, 2).reshape(nt * nb, D_MODEL)
    st_tm = state_pool[l].transpose(1, 0, 2)
    k_s, v_s, pool_s, q_s, cq_s, sgb_s, sgc_s, mixa_s = _inproj_sample(x_tm, st_tm, ng, w_in_b, gq, gk, gcq, pw, ps)

    def seq_major(a, rows):
        a = a.reshape(nt, nb, SEG).transpose(1, 0, 2)
        return a if rows == nt else jnp.pad(a, ((0, 0), (0, rows - nt), (0, 0)))

    kn = seq_major(k_s, nt)
    vn = seq_major(v_s, nt)

    nqb = s // ATT_TILE
    n_tile_steps = b * nqb * (nqb + 1) // 2
    n_pages = page_table.shape[1]
    n_seq = n_tile_steps * PAGES_PER_TILE // n_pages
    assert n_seq >= nb and n_seq * n_pages == n_tile_steps * PAGES_PER_TILE
    sin = jnp.concatenate([seq_major(q_s, r8), seq_major(cq_s, r8), seq_major(sgb_s, r8), seq_major(sgc_s, r8),
                           jnp.pad(kn, ((0, 0), (0, r8 - nt), (0, 0))), jnp.pad(vn, ((0, 0), (0, r8 - nt), (0, 0)))],
                          axis=1)
    sin = jnp.pad(sin, ((0, n_seq - nb), (0, 0), (0, 0)))
    n_pool = cache_k.shape[1]
    cache_kt = jnp.transpose(cache_k[l], (0, 2, 3, 4, 1)).reshape(n_pool, SEG, PAGE)
    cache_v3 = cache_v[l].reshape(n_pool, PAGE * HEADS, HEAD_W)
    mem_k3 = cache_mem_k[l].reshape(nb, N_MEM * HEADS, HEAD_W)
    mem_v3 = cache_mem_v[l].reshape(nb, N_MEM * HEADS, HEAD_W)
    y_p, sres = _attn(page_table, rel_flat, x_prompt, qb, kb, vb, cqb, sgb, sgc, mixa, mkb, mvb, w_out_b, hg,
                      lam_q[l], lam_k[l], sin, mem_k3, mem_v3, cache_kt, cache_v3, nt)
    mixd = sres[:nb, 0:r8]
    mixc = sres[:nb, r8:2 * r8]

    x8 = jnp.pad(x_sample, ((0, 0), (0, r8 - nt), (0, 0))).reshape(nb * r8, D_MODEL)
    y8 = _outproj(x8, seq_major(mixa_s, r8).reshape(nb * r8, SEG), mixd.reshape(nb * r8, SEG),
                  mixc.reshape(nb * r8, SEG), w_out_b)
    y_s = y8.reshape(nb, r8, D_MODEL)[:, :nt]

    return (y_p, y_s,
            k_p.reshape(1, b, s, HEADS, 2, MAP_W), v_p.reshape(1, b, s, HEADS, HEAD_W),
            pool_p[None], mk.reshape(1, b, N_MEM, HEADS, HEAD_W), mv.reshape(1, b, N_MEM, HEADS, HEAD_W),
            kn.reshape(1, nb, nt, HEADS, 2, MAP_W), vn.reshape(1, nb, nt, HEADS, HEAD_W),
            pool_s.transpose(1, 0, 2)[None])
```

```python
import functools
import math

import jax
import jax.numpy as jnp
from jax import lax
from jax.experimental import pallas as pl
from jax.experimental.pallas import tpu as pltpu

F32 = jnp.float32
BF16 = jnp.bfloat16

D_MODEL = 1024
SEG = 512
N_SEG = 8
POOL_WINDOWS = (2, 4, 8, 16)
POOL_PAD = 15
HEADS = 4
HEAD_W = 128
MAP_W = 64
N_MEM = 256
PAGE = 128
NUM_BUCKETS = 32
EPS = 1e-6
NEG_INF = -1e30
LAM_INIT = 0.8 - 0.6 * math.exp(-0.3 * 0)
DIFF_SCALE = MAP_W ** -0.5
CROSS_SCALE = HEAD_W ** -0.5
LOG2E = math.log2(math.e)

ROW_TILE = 512
ATT_TILE = 256
SAMPLE_ROWS = 8
PAGES_PER_TILE = 8
LOOKAHEAD = 3
RING = LOOKAHEAD + 2
SIN_ROWS = 6 * SAMPLE_ROWS
VMEM_LIMIT = 48 * 1024 * 1024
VMEM_LIMIT_ATTN = 56 * 1024 * 1024

_NT = (((1,), (1,)), ((), ()))


def _rms(x, g):
    ms = jnp.mean(x * x, axis=-1, keepdims=True)
    return x * lax.rsqrt(ms + EPS) * g


def _silu(x):
    return x * jax.nn.sigmoid(x)


def _norm_block64(blk, g):
    lo = lax.broadcasted_iota(jnp.int32, (1, HEAD_W), 1) < MAP_W
    sq = blk * blk
    s_lo = jnp.sum(jnp.where(lo, sq, 0.0), axis=-1, keepdims=True)
    s_hi = jnp.sum(jnp.where(lo, 0.0, sq), axis=-1, keepdims=True)
    r = jnp.where(lo, lax.rsqrt(s_lo * (1.0 / MAP_W) + EPS), lax.rsqrt(s_hi * (1.0 / MAP_W) + EPS))
    return blk * r * g


def _norm_block128(blk, g):
    ms = jnp.mean(blk * blk, axis=-1, keepdims=True)
    return blk * lax.rsqrt(ms + EPS) * g


def _lam(lq_ref, lk_ref):
    e = jnp.exp(jnp.sum(lq_ref[...] * lk_ref[...], axis=-1, keepdims=True))
    return e[0:1, :] - e[1:2, :] + LAM_INIT


def _shifted_bias(dist, rel_ref, h):
    n = jnp.maximum(dist, 0)
    nf = jnp.maximum(n, 1).astype(F32)
    large = 16 + (jnp.log(nf / 16) / math.log(128 / 16) * 16).astype(jnp.int32)
    large = jnp.minimum(large, NUM_BUCKETS - 1)
    bucket = jnp.where(n < 16, n, large)
    far = rel_ref[(NUM_BUCKETS - 1) * HEADS + h]
    out = jnp.zeros(dist.shape, F32)
    for b in range(NUM_BUCKETS - 1):
        out = jnp.where(bucket == b, rel_ref[b * HEADS + h] - far, out)
    return out


def _project_segments(h, w_ref, gq_ref, gk_ref, gcq_ref, k_ref, v_ref, qb_ref, kb_ref, vb_ref,
                      cqb_ref, sgb_ref, sgc_ref, q_scale, v_head_rows):
    rows = h.shape[0]

    def seg(s):
        return jnp.dot(h, w_ref[:, s * SEG:(s + 1) * SEG], preferred_element_type=F32)

    q = seg(2)
    for c in range(HEADS):
        sl = slice(c * HEAD_W, (c + 1) * HEAD_W)
        qb_ref[:, sl] = (_norm_block64(q[:, sl], gq_ref[:, sl]) * q_scale).astype(qb_ref.dtype)
    k = seg(3)
    for c in range(HEADS):
        sl = slice(c * HEAD_W, (c + 1) * HEAD_W)
        kn = _norm_block64(k[:, sl], gk_ref[:, sl])
        k_ref[:, sl] = kn
        if kb_ref is not None:
            kb_ref[:, sl] = kn.astype(BF16)
    v = seg(4)
    if v_head_rows:
        for c in range(HEADS):
            v_ref[pl.ds(c, rows, stride=HEADS), :] = v[:, c * HEAD_W:(c + 1) * HEAD_W]
    else:
        v_ref[...] = v
    if vb_ref is not None:
        vb_ref[...] = v.astype(BF16)
    sgb_ref[...] = _silu(seg(5)).astype(sgb_ref.dtype)
    cq = seg(6)
    for c in range(HEADS):
        sl = slice(c * HEAD_W, (c + 1) * HEAD_W)
        cqb_ref[:, sl] = _norm_block128(cq[:, sl], gcq_ref[:, sl]).astype(cqb_ref.dtype)
    sgc_ref[...] = _silu(seg(7)).astype(sgc_ref.dtype)


def _inproj_prompt_kernel(x_ref, ng_ref, w_ref, gq_ref, gk_ref, gcq_ref, pw_ref, ps_ref,
                          k_ref, v_ref, pool_ref, qb_ref, kb_ref, vb_ref, cqb_ref, sgb_ref, sgc_ref,
                          mixa_ref, uext_ref):
    tm = ROW_TILE
    i = pl.program_id(1)

    @pl.when(i == 0)
    def _():
        uext_ref[0:16, :] = jnp.zeros((16, SEG), F32)

    h = _rms(x_ref[...], ng_ref[...]).astype(BF16)
    u = jnp.dot(h, w_ref[:, 0:SEG], preferred_element_type=F32)
    uext_ref[16:16 + tm, :] = u
    sga = _silu(jnp.dot(h, w_ref[:, SEG:2 * SEG], preferred_element_type=F32))
    pos = i * tm + lax.broadcasted_iota(jnp.int32, (tm, 1), 0)
    for g, win in enumerate(POOL_WINDOWS):
        sl = slice(g * HEAD_W, (g + 1) * HEAD_W)
        ug = u[:, sl]
        acc = ug
        for j in range(1, win):
            acc = acc + uext_ref[16 - j:16 - j + tm, sl]
        cnt = jnp.minimum(pos + 1, win).astype(F32)
        d = acc / cnt - ug
        y = jnp.dot(d.astype(BF16), pw_ref[g], preferred_element_type=F32) * ps_ref[:, sl]
        mixa_ref[:, sl] = (y * sga[:, sl]).astype(BF16)

    @pl.when(i == pl.num_programs(1) - 1)
    def _():
        pool_ref[...] = uext_ref[tm + 1:tm + 16, :]

    uext_ref[0:16, :] = uext_ref[tm:tm + 16, :]

    _project_segments(h, w_ref, gq_ref, gk_ref, gcq_ref, k_ref, v_ref, qb_ref, kb_ref, vb_ref,
                      cqb_ref, sgb_ref, sgc_ref, q_scale=DIFF_SCALE * LOG2E, v_head_rows=True)


def _inproj_prompt(x, ng, w_in, gq, gk, gcq, pw, ps):
    b, s, _ = x.shape
    tm = ROW_TILE
    row = lambda width: pl.BlockSpec((None, tm, width), lambda bi, i: (bi, i, 0))
    const = lambda shape: pl.BlockSpec(shape, lambda bi, i: (0,) * len(shape))
    out_shape = (
        jax.ShapeDtypeStruct((b, s, SEG), F32),
        jax.ShapeDtypeStruct((b, s * HEADS, HEAD_W), F32),
        jax.ShapeDtypeStruct((b, POOL_PAD, SEG), F32),
        jax.ShapeDtypeStruct((b, s, SEG), BF16),
        jax.ShapeDtypeStruct((b, s, SEG), BF16),
        jax.ShapeDtypeStruct((b, s, SEG), BF16),
        jax.ShapeDtypeStruct((b, s, SEG), BF16),
        jax.ShapeDtypeStruct((b, s, SEG), BF16),
        jax.ShapeDtypeStruct((b, s, SEG), BF16),
        jax.ShapeDtypeStruct((b, s, SEG), BF16),
    )
    out_specs = (row(SEG), pl.BlockSpec((None, tm * HEADS, HEAD_W), lambda bi, i: (bi, i, 0)),
                 pl.BlockSpec((None, POOL_PAD, SEG), lambda bi, i: (bi, 0, 0)),
                 row(SEG), row(SEG), row(SEG), row(SEG), row(SEG), row(SEG), row(SEG))
    return pl.pallas_call(
        _inproj_prompt_kernel,
        out_shape=out_shape,
        grid=(b, s // tm),
        in_specs=[row(D_MODEL), const((1, D_MODEL)), const((D_MODEL, N_SEG * SEG)),
                  const((1, SEG)), const((1, SEG)), const((1, SEG)),
                  const((len(POOL_WINDOWS), HEAD_W, HEAD_W)), const((1, SEG))],
        out_specs=out_specs,
        scratch_shapes=[pltpu.VMEM((tm + 16, SEG), F32)],
        compiler_params=pltpu.CompilerParams(dimension_semantics=("arbitrary", "arbitrary"),
                                             vmem_limit_bytes=VMEM_LIMIT),
        name="inproj_prompt",
    )(x, ng, w_in, gq, gk, gcq, pw, ps)


def _memkv_kernel(mem_ref, mg_ref, w_ref, gck_ref, mk_ref, mv_ref, mkb_ref, mvb_ref):
    h = _rms(mem_ref[...], mg_ref[...]).astype(BF16)
    k = jnp.dot(h, w_ref[:, 0:SEG], preferred_element_type=F32)
    for c in range(HEADS):
        sl = slice(c * HEAD_W, (c + 1) * HEAD_W)
        kn = _norm_block128(k[:, sl], gck_ref[:, sl])
        mk_ref[:, sl] = kn
        mkb_ref[:, sl] = kn.astype(BF16)
    v = jnp.dot(h, w_ref[:, SEG:2 * SEG], preferred_element_type=F32)
    mv_ref[...] = v
    mvb_ref[...] = v.astype(BF16)


def _memkv(mem, mg, w_kv, gck):
    b = mem.shape[0]
    blk = lambda width: pl.BlockSpec((None, N_MEM, width), lambda bi: (bi, 0, 0))
    const = lambda shape: pl.BlockSpec(shape, lambda bi: (0,) * len(shape))
    return pl.pallas_call(
        _memkv_kernel,
        out_shape=(jax.ShapeDtypeStruct((b, N_MEM, SEG), F32), jax.ShapeDtypeStruct((b, N_MEM, SEG), F32),
                   jax.ShapeDtypeStruct((b, N_MEM, SEG), BF16), jax.ShapeDtypeStruct((b, N_MEM, SEG), BF16)),
        grid=(b,),
        in_specs=[blk(D_MODEL), const((1, D_MODEL)), const((D_MODEL, 2 * SEG)), const((1, SEG))],
        out_specs=(blk(SEG), blk(SEG), blk(SEG), blk(SEG)),
        compiler_params=pltpu.CompilerParams(dimension_semantics=("arbitrary",),
                                             vmem_limit_bytes=VMEM_LIMIT),
        name="mem_kv",
    )(mem, mg, w_kv, gck)


def _attn_kernel(pt_ref, rel_ref, x_ref, qb_ref, kb_ref, vb_ref, cqb_ref, sgb_ref, sgc_ref, mixa_ref,
                 mkb_ref, mvb_ref, wout_ref, hg_ref, lq_ref, lk_ref,
                 sin_hbm, memk_hbm, memv_hbm, kt_hbm, v_hbm,
                 o_ref, sout_hbm,
                 bdiag_ref, bsub_ref, qq_ref, m_ref, l_ref, acc_ref, mix_ref,
                 kbuf, vbuf, ring_sem, sin_buf, mk_buf, mv_buf, fetch_sem, stage, out_sem,
                 qbd_ref, sm_ref, sl_ref, sacc_ref, blast_ref, bnew_ref, kpad_ref, vpad_ref,
                 *, n_seq, n_mem_seq, n_new):
    t = ATT_TILE
    npg = PAGES_PER_TILE
    r8 = SAMPLE_ROWS
    hr = 2 * r8
    groups_per_seq = pt_ref.shape[1] // npg
    assert groups_per_seq & (groups_per_seq - 1) == 0
    seq_shift = groups_per_seq.bit_length() - 1
    bi = pl.program_id(0)
    qi = pl.program_id(1)
    nqb = kb_ref.shape[0] // t
    first_step = (bi == 0) & (qi == 0)
    last_step = (bi == pl.num_programs(0) - 1) & (qi == pl.num_programs(1) - 1)
    u_base = bi * (nqb * (nqb + 1) // 2) + jnp.right_shift(qi * (qi + 1), 1)

    lane = lax.broadcasted_iota(jnp.int32, (1, SEG), 1)

    def static(v):
        return isinstance(v, int)

    def ring_slot(pos):
        return pos % RING if static(pos) else lax.rem(pos, jnp.int32(RING))

    def clamp(v, hi):
        return min(v, hi) if static(v) else jnp.minimum(v, hi)

    def ring_copies(group, slot):
        out = []
        for i in range(npg):
            if group is None:
                pid = 0
            else:
                pid = pt_ref[jnp.right_shift(group, seq_shift), (group & (groups_per_seq - 1)) * npg + i]
            out.append(pltpu.make_async_copy(kt_hbm.at[pid], kbuf.at[slot, i], ring_sem.at[slot]))
            out.append(pltpu.make_async_copy(v_hbm.at[pid], vbuf.at[slot, i], ring_sem.at[slot]))
        return out

    n_real_groups = n_mem_seq * groups_per_seq

    def ring_start(pos):
        def go():
            for n, cp in enumerate(ring_copies(pos, ring_slot(pos))):
                cp.start(priority=n % 2)
        if static(pos):
            assert pos < n_real_groups
            go()
        else:
            pl.when(pos < n_real_groups)(go)

    def ring_wait(pos):
        def go():
            for cp in ring_copies(None, ring_slot(pos)):
                cp.wait()
        pl.when(pos < n_real_groups)(go)

    def fetch_copies(seq):
        slot = seq & 1
        s_in = clamp(seq, n_seq - 1)
        s_mem = clamp(seq, n_mem_seq - 1)
        return (pltpu.make_async_copy(sin_hbm.at[s_in], sin_buf.at[slot], fetch_sem.at[slot]),
                pltpu.make_async_copy(memk_hbm.at[s_mem], mk_buf.at[slot], fetch_sem.at[slot]),
                pltpu.make_async_copy(memv_hbm.at[s_mem], mv_buf.at[slot], fetch_sem.at[slot]))

    def out_copy(row, slot):
        return pltpu.make_async_copy(stage.at[slot], sout_hbm.at[row], out_sem.at[slot])

    def sample_begin(slot):
        q8 = sin_buf[slot, 0:r8, :].astype(BF16)
        zero = jnp.zeros_like(q8)
        for hm in range(2 * HEADS):
            qbd_ref[hm * r8:(hm + 1) * r8, :] = jnp.where(
                (lane >= hm * MAP_W) & (lane < (hm + 1) * MAP_W), q8, zero)
        sm_ref[...] = jnp.full(sm_ref.shape, NEG_INF, F32)
        sl_ref[...] = jnp.zeros(sl_ref.shape, F32)
        sacc_ref[...] = jnp.zeros(sacc_ref.shape, F32)

    def sample_attend(s_parts, v_of):
        n = len(s_parts)
        s = jnp.concatenate(s_parts, axis=1) if n > 1 else s_parts[0]
        m_old = sm_ref[...]
        m_new = jnp.maximum(m_old, jnp.max(s, axis=-1, keepdims=True))
        alpha = jnp.exp(m_old - m_new)
        p = jnp.exp(s - (jnp.concatenate([m_new] * n, axis=1) if n > 1 else m_new))
        psum = p[:, 0:PAGE]
        for i in range(1, n):
            psum = psum + p[:, i * PAGE:(i + 1) * PAGE]
        sl_ref[...] = alpha * sl_ref[...] + psum
        pb = p.astype(BF16)
        heads = []
        for h in range(HEADS):
            pv = None
            for i in range(n):
                part = jnp.dot(pb[h * hr:(h + 1) * hr, i * PAGE:(i + 1) * PAGE], v_of(i, h),
                               preferred_element_type=F32)
                pv = part if pv is None else pv + part
            heads.append(pv)
        sacc_ref[...] = alpha * sacc_ref[...] + jnp.concatenate(heads, axis=0)
        sm_ref[...] = m_new

    def sample_scores(u):
        slot = ring_slot(u)
        is_last_group = (u & (groups_per_seq - 1)) == groups_per_seq - 1
        last_bias = jnp.where(is_last_group, blast_ref[...], 0.0)
        qbd = qbd_ref[...]
        s_parts = []
        for i in range(npg):
            s = jnp.dot(qbd, kbuf[slot, i].astype(BF16), preferred_element_type=F32)
            if i == npg - 1:
                s = s + last_bias
            s_parts.append(s)
        return s_parts

    def sample_values(u, s_parts):
        slots = [ring_slot(u + g) for g in range(len(s_parts) // npg)]
        sample_attend(s_parts, lambda i, h: vbuf[slots[i // npg], i % npg,
                                                 pl.ds(h, PAGE, stride=HEADS), :].astype(BF16))

    def sample_end(u):
        seq = jnp.right_shift(u, seq_shift)
        slot = seq & 1
        out_copy(0, slot).wait()

        cq8 = sin_buf[slot, r8:2 * r8, :].astype(BF16)
        cross_scores = []
        for h in range(HEADS):
            mk_h = mk_buf[slot, pl.ds(h, N_MEM, stride=HEADS), :].astype(BF16)
            cross_scores.append(lax.dot_general(cq8[:, h * HEAD_W:(h + 1) * HEAD_W], mk_h, _NT,
                                                preferred_element_type=F32) * CROSS_SCALE)
        kpad_ref[0:r8, :] = sin_buf[slot, 4 * r8:5 * r8, :].astype(BF16)
        vpad_ref[0:r8, :] = sin_buf[slot, 5 * r8:6 * r8, :].astype(BF16)
        s_new = lax.dot_general(qbd_ref[...], kpad_ref[...], _NT, preferred_element_type=F32) + bnew_ref[...]
        sample_attend([s_new], lambda i, h: vpad_ref[:, h * HEAD_W:(h + 1) * HEAD_W])

        cross_p = [jnp.exp(sc - jnp.max(sc, axis=-1, keepdims=True)) for sc in cross_scores]
        lam = _lam(lq_ref, lk_ref)
        o = sacc_ref[...] / jnp.sum(sl_ref[...], axis=-1, keepdims=True)
        for h in range(HEADS):
            sl = slice(h * HEAD_W, (h + 1) * HEAD_W)
            mv_h = mv_buf[slot, pl.ds(h, N_MEM, stride=HEADS), :].astype(BF16)
            oc = jnp.dot(cross_p[h].astype(BF16), mv_h, preferred_element_type=F32)
            oc = oc / jnp.sum(cross_p[h], axis=-1, keepdims=True)
            stage[slot, r8:2 * r8, sl] = oc * sin_buf[slot, 3 * r8:4 * r8, sl]
            a = o[h * hr:h * hr + r8, :] - lam * o[h * hr + r8:(h + 1) * hr, :]
            od = _norm_block128(a, hg_ref[:, sl]) * (1.0 - LAM_INIT)
            stage[slot, 0:r8, sl] = od * sin_buf[slot, 2 * r8:3 * r8, sl]
        out_copy(seq, slot).start()
        for cp in fetch_copies(seq + 1):
            cp.wait()
        sample_begin(1 - slot)
        for cp in fetch_copies(seq + 2):
            cp.start()

    @pl.when(first_step)
    def _():
        for g in range(LOOKAHEAD):
            ring_start(g)
        for cp in fetch_copies(0):
            cp.start()
        d = lax.broadcasted_iota(jnp.int32, (t, t), 0) - lax.broadcasted_iota(jnp.int32, (t, t), 1)
        for h in range(HEADS):
            bdiag_ref[h] = jnp.where(d >= 0, _shifted_bias(d, rel_ref, h) * LOG2E, NEG_INF)
            bsub_ref[h] = _shifted_bias(d + t, rel_ref, h) * LOG2E
        tok = lax.broadcasted_iota(jnp.int32, (hr, PAGE), 0) & (r8 - 1)
        key = lax.broadcasted_iota(jnp.int32, (hr, PAGE), 1)
        for h in range(HEADS):
            blast_ref[h * hr:(h + 1) * hr, :] = _shifted_bias(PAGE + tok - key, rel_ref, h)
            ok = (key <= tok) & (key < n_new)
            bnew_ref[h * hr:(h + 1) * hr, :] = jnp.where(ok, _shifted_bias(tok - key, rel_ref, h), NEG_INF)
        kpad_ref[...] = jnp.zeros(kpad_ref.shape, BF16)
        vpad_ref[...] = jnp.zeros(vpad_ref.shape, BF16)
        stage[...] = jnp.zeros(stage.shape, F32)
        out_copy(n_seq, 0).start()
        out_copy(n_seq + 1, 1).start()
        for cp in fetch_copies(0):
            cp.wait()
        sample_begin(0)
        for cp in fetch_copies(1):
            cp.start()

    lam = _lam(lq_ref, lk_ref)
    lo = lax.broadcasted_iota(jnp.int32, (1, HEAD_W), 1) < MAP_W
    mix_ref[:, 0:SEG] = mixa_ref[...]

    for h in range(HEADS):
        qh = qb_ref[:, h * HEAD_W:(h + 1) * HEAD_W]
        zero = jnp.zeros_like(qh)
        qq_ref[h, 0:t, :] = jnp.where(lo, qh, zero)
        qq_ref[h, t:2 * t, :] = jnp.where(lo, zero, qh)
    m_ref[...] = jnp.full(m_ref.shape, NEG_INF, F32)
    l_ref[...] = jnp.zeros(l_ref.shape, F32)
    acc_ref[...] = jnp.zeros(acc_ref.shape, F32)

    def tile_step(j0, bias_ref, u, n):
        w = n * t
        for g in range(n):
            ring_start(u + LOOKAHEAD + g)
        for g in range(n):
            ring_wait(u + g)
        s_parts = []
        for g in range(n):
            s_parts += sample_scores(u + g)
        def head_scores(h):
            s = lax.dot_general(qq_ref[h], kb_ref[pl.ds(j0, w), h * HEAD_W:(h + 1) * HEAD_W], _NT,
                                preferred_element_type=F32)
            if bias_ref is not None:
                bias = bias_ref[h]
                s = s + jnp.concatenate([bias, bias], axis=0)
            return s

        scores = [head_scores(h) for h in range(HEADS)] if n > 1 else None
        for h in range(HEADS):
            sl = slice(h * HEAD_W, (h + 1) * HEAD_W)
            s = scores[h] if n > 1 else head_scores(h)
            m_old = m_ref[h]
            m_new = jnp.maximum(m_old, jnp.max(s, axis=-1, keepdims=True))
            alpha = jnp.exp2(m_old - m_new)
            p = jnp.exp2(s - jnp.concatenate([m_new] * (w // HEAD_W), axis=1))
            psum = p[:, 0:HEAD_W]
            for c in range(1, w // HEAD_W):
                psum = psum + p[:, c * HEAD_W:(c + 1) * HEAD_W]
            l_ref[h] = alpha * l_ref[h] + psum
            acc_ref[h] = alpha * acc_ref[h] + jnp.dot(p.astype(BF16), vb_ref[pl.ds(j0, w), sl],
                                                      preferred_element_type=F32)
            m_ref[h] = m_new
        sample_values(u, s_parts)

        u_last = u + n - 1

        @pl.when((u_last & (groups_per_seq - 1)) == groups_per_seq - 1)
        def _():
            sample_end(u_last)

    tile_step(pl.multiple_of(qi * t, t), bdiag_ref, u_base, 1)

    @pl.when(qi >= 1)
    def _():
        tile_step(pl.multiple_of((qi - 1) * t, t), bsub_ref, u_base + 1, 1)

    n_far = jnp.maximum(qi - 1, 0)
    u_far = u_base + 2
    lead = jnp.where((n_far >= 1) & ((u_far & 1) == 1), 1, 0)
    n_pairs = jnp.right_shift(n_far - lead, 1)
    trail = (n_far - lead) & 1

    @pl.when(lead == 1)
    def _():
        tile_step(0, None, u_far, 1)

    def far_pair(k, carry):
        j = lead + 2 * k
        tile_step(pl.multiple_of(j * t, t), None, u_far + j, 2)
        return carry

    lax.fori_loop(0, n_pairs, far_pair, 0)

    @pl.when(trail == 1)
    def _():
        tile_step(pl.multiple_of((n_far - 1) * t, t), None, u_far + n_far - 1, 1)

    cross_scores = [lax.dot_general(cqb_ref[:, h * HEAD_W:(h + 1) * HEAD_W], mkb_ref[:, h * HEAD_W:(h + 1) * HEAD_W],
                                    _NT, preferred_element_type=F32) * CROSS_SCALE for h in range(HEADS)]
    for h in range(HEADS):
        sl = slice(h * HEAD_W, (h + 1) * HEAD_W)
        o = acc_ref[h] / jnp.sum(l_ref[h], axis=-1, keepdims=True)
        a = o[0:t, :] - lam * o[t:2 * t, :]
        od = _norm_block128(a, hg_ref[:, sl]) * (1.0 - LAM_INIT)
        mix_ref[:, SEG + h * HEAD_W:SEG + (h + 1) * HEAD_W] = (od * sgb_ref[:, sl].astype(F32)).astype(BF16)

        sc = cross_scores[h]
        pc = jnp.exp(sc - jnp.max(sc, axis=-1, keepdims=True))
        c = jnp.dot(pc.astype(BF16), mvb_ref[:, sl], preferred_element_type=F32)
        c = c / jnp.sum(pc, axis=-1, keepdims=True)
        mix_ref[:, 2 * SEG + h * HEAD_W:2 * SEG + (h + 1) * HEAD_W] = (c * sgc_ref[:, sl].astype(F32)).astype(BF16)

    o_ref[...] = x_ref[...] + jnp.dot(mix_ref[...], wout_ref[...], preferred_element_type=F32)

    @pl.when(last_step)
    def _():
        for cp in fetch_copies(n_seq + 1):
            cp.wait()
        out_copy(0, 0).wait()
        out_copy(0, 1).wait()


def _attn(page_table, rel_flat, x, qb, kb, vb, cqb, sgb, sgc, mixa, mkb, mvb, w_out, hg, lq, lk,
          sin, mem_k3, mem_v3, cache_kt, cache_v3, n_new):
    b, s, _ = x.shape
    t = ATT_TILE
    nqb = s // t
    n_tile_steps = b * nqb * (nqb + 1) // 2
    n_seq = sin.shape[0]
    assert n_tile_steps * PAGES_PER_TILE == n_seq * page_table.shape[1]
    assert sin.shape == (n_seq, SIN_ROWS, SEG) and page_table.shape[0] == mem_k3.shape[0]
    row = lambda width: pl.BlockSpec((None, t, width), lambda bi, i, pt: (bi, i, 0))
    seq = lambda n: pl.BlockSpec((None, n, SEG), lambda bi, i, pt: (bi, 0, 0), pipeline_mode=pl.Buffered(1))
    const = lambda shape: pl.BlockSpec(shape, lambda bi, i, pt: (0,) * len(shape))
    hbm = pl.BlockSpec(memory_space=pl.ANY)
    r8 = SAMPLE_ROWS
    nq = HEADS * 2 * r8
    grid_spec = pltpu.PrefetchScalarGridSpec(
        num_scalar_prefetch=1,
        grid=(b, nqb),
        in_specs=[pl.BlockSpec(memory_space=pltpu.SMEM),
                  row(D_MODEL), row(SEG), seq(s), seq(s), row(SEG), row(SEG), row(SEG), row(SEG),
                  seq(N_MEM), seq(N_MEM), const((3 * SEG, D_MODEL)), const((1, SEG)),
                  const((2, MAP_W)), const((2, MAP_W)), hbm, hbm, hbm, hbm, hbm],
        out_specs=(row(D_MODEL), hbm),
        scratch_shapes=[
            pltpu.VMEM((HEADS, t, t), F32), pltpu.VMEM((HEADS, t, t), F32),
            pltpu.VMEM((HEADS, 2 * t, HEAD_W), BF16),
            pltpu.VMEM((HEADS, 2 * t, HEAD_W), F32), pltpu.VMEM((HEADS, 2 * t, HEAD_W), F32),
            pltpu.VMEM((HEADS, 2 * t, HEAD_W), F32), pltpu.VMEM((t, 3 * SEG), BF16),
            pltpu.VMEM((RING, PAGES_PER_TILE, SEG, PAGE), F32),
            pltpu.VMEM((RING, PAGES_PER_TILE, PAGE * HEADS, HEAD_W), F32),
            pltpu.SemaphoreType.DMA((RING,)),
            pltpu.VMEM((2, SIN_ROWS, SEG), F32), pltpu.VMEM((2, N_MEM * HEADS, HEAD_W), F32),
            pltpu.VMEM((2, N_MEM * HEADS, HEAD_W), F32), pltpu.SemaphoreType.DMA((2,)),
            pltpu.VMEM((2, 2 * r8, SEG), F32), pltpu.SemaphoreType.DMA((2,)),
            pltpu.VMEM((nq, SEG), BF16),
            pltpu.VMEM((nq, HEAD_W), F32), pltpu.VMEM((nq, HEAD_W), F32), pltpu.VMEM((nq, HEAD_W), F32),
            pltpu.VMEM((nq, PAGE), F32), pltpu.VMEM((nq, PAGE), F32),
            pltpu.VMEM((PAGE, SEG), BF16), pltpu.VMEM((PAGE, SEG), BF16)],
    )
    kern = functools.partial(_attn_kernel, n_seq=n_seq, n_mem_seq=mem_k3.shape[0], n_new=n_new)
    return pl.pallas_call(
        kern,
        out_shape=(jax.ShapeDtypeStruct((b, s, D_MODEL), F32),
                   jax.ShapeDtypeStruct((n_seq + 2, 2 * r8, SEG), F32)),
        grid_spec=grid_spec,
        compiler_params=pltpu.CompilerParams(dimension_semantics=("arbitrary", "arbitrary"),
                                             vmem_limit_bytes=VMEM_LIMIT_ATTN),
        name="attn",
    )(page_table, rel_flat, x, qb, kb, vb, cqb, sgb, sgc, mixa, mkb, mvb, w_out, hg, lq, lk,
      sin, mem_k3, mem_v3, cache_kt, cache_v3)


def _inproj_sample_kernel(x_ref, st_ref, ng_ref, w_ref, gq_ref, gk_ref, gcq_ref, pw_ref, ps_ref,
                          k_ref, v_ref, pool_ref, q_ref, cq_ref, sgb_ref, sgc_ref, mixa_ref, d_ref):
    nb = st_ref.shape[1]
    nt = x_ref.shape[0] // nb
    h = _rms(x_ref[...], ng_ref[...]).astype(BF16)
    u = jnp.dot(h, w_ref[:, 0:SEG], preferred_element_type=F32)
    sga = _silu(jnp.dot(h, w_ref[:, SEG:2 * SEG], preferred_element_type=F32))

    def ext(e, sl):
        if e < POOL_PAD:
            return st_ref[e, :, sl]
        return u[(e - POOL_PAD) * nb:(e - POOL_PAD + 1) * nb, sl]

    for g, win in enumerate(POOL_WINDOWS):
        sl = slice(g * HEAD_W, (g + 1) * HEAD_W)
        for tk in range(nt):
            acc = ext(POOL_PAD + tk, sl)
            for j in range(1, win):
                acc = acc + ext(POOL_PAD + tk - j, sl)
            d_ref[tk * nb:(tk + 1) * nb, sl] = acc / float(win) - ext(POOL_PAD + tk, sl)
        y = jnp.dot(d_ref[:, sl].astype(BF16), pw_ref[g], preferred_element_type=F32) * ps_ref[:, sl]
        mixa_ref[:, sl] = y * sga[:, sl]

    keep = POOL_PAD - nt
    pool_ref[0:keep] = st_ref[nt:POOL_PAD]
    for tk in range(nt):
        pool_ref[keep + tk] = u[tk * nb:(tk + 1) * nb, :]

    _project_segments(h, w_ref, gq_ref, gk_ref, gcq_ref, k_ref, v_ref, q_ref, None, None,
                      cq_ref, sgb_ref, sgc_ref, q_scale=DIFF_SCALE, v_head_rows=False)


def _inproj_sample(x_tm, state_tm, ng, w_in, gq, gk, gcq, pw, ps):
    rows = x_tm.shape[0]
    nb = state_tm.shape[1]
    vmem = pl.BlockSpec(memory_space=pltpu.VMEM)
    mat = jax.ShapeDtypeStruct((rows, SEG), F32)
    return pl.pallas_call(
        _inproj_sample_kernel,
        out_shape=(mat, mat, jax.ShapeDtypeStruct((POOL_PAD, nb, SEG), F32), mat, mat, mat, mat, mat),
        in_specs=[vmem] * 9,
        out_specs=(vmem,) * 8,
        scratch_shapes=[pltpu.VMEM((rows, SEG), F32)],
        compiler_params=pltpu.CompilerParams(vmem_limit_bytes=VMEM_LIMIT),
        name="inproj_sample",
    )(x_tm, state_tm, ng, w_in, gq, gk, gcq, pw, ps)


def _outproj_kernel(x_ref, a_ref, d_ref, c_ref, w_ref, o_ref):
    y = jnp.dot(a_ref[...].astype(BF16), w_ref[0:SEG, :], preferred_element_type=F32)
    y = y + jnp.dot(d_ref[...].astype(BF16), w_ref[SEG:2 * SEG, :], preferred_element_type=F32)
    y = y + jnp.dot(c_ref[...].astype(BF16), w_ref[2 * SEG:3 * SEG, :], preferred_element_type=F32)
    o_ref[...] = x_ref[...] + y


def _outproj(x, a, d, c, w_out):
    rows = x.shape[0]
    vmem = pl.BlockSpec(memory_space=pltpu.VMEM)
    return pl.pallas_call(
        _outproj_kernel,
        out_shape=jax.ShapeDtypeStruct((rows, D_MODEL), F32),
        in_specs=[vmem] * 5,
        out_specs=vmem,
        compiler_params=pltpu.CompilerParams(vmem_limit_bytes=VMEM_LIMIT),
        name="outproj_sample",
    )(x, a, d, c, w_out)


def _tile_gain(g, reps):
    return jnp.tile(g.reshape(1, -1), (1, reps))


def kernel(x_prompt, x_sample, mem_prompt, cache_k, cache_v, cache_mem_k, cache_mem_v, state_pool, page_table,
           norm_g, w_in, q_norm_g, k_norm_g, cq_norm_g, ck_norm_g, mem_norm_g, w_mem_kv, lam_q, lam_k,
           head_norm_g, pool_w, pool_scale, rel_bias, w_out):
    depth = w_in.shape[0]
    assert depth == 1
    l = 0
    b, s, _ = x_prompt.shape
    nb, nt, _ = x_sample.shape
    r8 = SAMPLE_ROWS

    ng = norm_g[l].reshape(1, D_MODEL)
    w_in_b = w_in[l].astype(BF16)
    gq = _tile_gain(q_norm_g[l], SEG // MAP_W)
    gk = _tile_gain(k_norm_g[l], SEG // MAP_W)
    gcq = _tile_gain(cq_norm_g[l], HEADS)
    gck = _tile_gain(ck_norm_g[l], HEADS)
    pw = pool_w[l].astype(BF16)
    ps = pool_scale[l].reshape(1, SEG)
    hg = head_norm_g[l].reshape(1, SEG)
    w_out_b = w_out[l].astype(BF16)
    rel_flat = rel_bias.reshape(-1)

    k_p, v_p, pool_p, qb, kb, vb, cqb, sgb, sgc, mixa = _inproj_prompt(x_prompt, ng, w_in_b, gq, gk, gcq, pw, ps)
    mk, mv, mkb, mvb = _memkv(mem_prompt, mem_norm_g[l].reshape(1, D_MODEL), w_mem_kv[l].astype(BF16), gck)
    x_tm = x_sample.transpose(1, 0, 2).reshape(nt * nb, D_MODEL)
    st_tm = state_pool[l].transpose(1, 0, 2)
    k_s, v_s, pool_s, q_s, cq_s, sgb_s, sgc_s, mixa_s = _inproj_sample(x_tm, st_tm, ng, w_in_b, gq, gk, gcq, pw, ps)

    def seq_major(a, rows):
        a = a.reshape(nt, nb, SEG).transpose(1, 0, 2)
        return a if rows == nt else jnp.pad(a, ((0, 0), (0, rows - nt), (0, 0)))

    kn = seq_major(k_s, nt)
    vn = seq_major(v_s, nt)

    nqb = s // ATT_TILE
    n_tile_steps = b * nqb * (nqb + 1) // 2
    n_pages = page_table.shape[1]
    n_seq = n_tile_steps * PAGES_PER_TILE // n_pages
    assert n_seq >= nb and n_seq * n_pages == n_tile_steps * PAGES_PER_TILE
    sin = jnp.concatenate([seq_major(q_s, r8), seq_major(cq_s, r8), seq_major(sgb_s, r8), seq_major(sgc_s, r8),
                           jnp.pad(kn, ((0, 0), (0, r8 - nt), (0, 0))), jnp.pad(vn, ((0, 0), (0, r8 - nt), (0, 0)))],
                          axis=1)
    sin = jnp.pad(sin, ((0, n_seq - nb), (0, 0), (0, 0)))
    n_pool = cache_k.shape[1]
    cache_kt = jnp.transpose(cache_k[l], (0, 2, 3, 4, 1)).reshape(n_pool, SEG, PAGE)
    cache_v3 = cache_v[l].reshape(n_pool, PAGE * HEADS, HEAD_W)
    mem_k3 = cache_mem_k[l].reshape(nb, N_MEM * HEADS, HEAD_W)
    mem_v3 = cache_mem_v[l].reshape(nb, N_MEM * HEADS, HEAD_W)
    y_p, sres = _attn(page_table, rel_flat, x_prompt, qb, kb, vb, cqb, sgb, sgc, mixa, mkb, mvb, w_out_b, hg,
                      lam_q[l], lam_k[l], sin, mem_k3, mem_v3, cache_kt, cache_v3, nt)
    mixd = sres[:nb, 0:r8]
    mixc = sres[:nb, r8:2 * r8]

    x8 = jnp.pad(x_sample, ((0, 0), (0, r8 - nt), (0, 0))).reshape(nb * r8, D_MODEL)
    y8 = _outproj(x8, seq_major(mixa_s, r8).reshape(nb * r8, SEG), mixd.reshape(nb * r8, SEG),
                  mixc.reshape(nb * r8, SEG), w_out_b)
    y_s = y8.reshape(nb, r8, D_MODEL)[:, :nt]

    return (y_p, y_s,
            k_p.reshape(1, b, s, HEADS, 2, MAP_W), v_p.reshape(1, b, s, HEADS, HEAD_W),
            pool_p[None], mk.reshape(1, b, N_MEM, HEADS, HEAD_W), mv.reshape(1, b, N_MEM, HEADS, HEAD_W),
            kn.reshape(1, nb, nt, HEADS, 2, MAP_W), vn.reshape(1, nb, nt, HEADS, HEAD_W),
            pool_s.transpose(1, 0, 2)[None])
```

```python
import functools
import math

import jax
import jax.numpy as jnp
from jax import lax
from jax.experimental import pallas as pl
from jax.experimental.pallas import tpu as pltpu

F32 = jnp.float32
BF16 = jnp.bfloat16

D_MODEL = 1024
SEG = 512
N_SEG = 8
POOL_WINDOWS = (2, 4, 8, 16)
POOL_PAD = 15
HEADS = 4
HEAD_W = 128
MAP_W = 64
N_MEM = 256
PAGE = 128
NUM_BUCKETS = 32
EPS = 1e-6
NEG_INF = -1e30
LAM_INIT = 0.8 - 0.6 * math.exp(-0.3 * 0)
DIFF_SCALE = MAP_W ** -0.5
CROSS_SCALE = HEAD_W ** -0.5
LOG2E = math.log2(math.e)

ROW_TILE = 1024
ATT_TILE = 256
SAMPLE_ROWS = 8
PAGES_PER_TILE = 8
LOOKAHEAD = 3
RING = LOOKAHEAD + 2
SIN_ROWS = 6 * SAMPLE_ROWS
VMEM_LIMIT = 48 * 1024 * 1024
VMEM_LIMIT_ATTN = 56 * 1024 * 1024

_NT = (((1,), (1,)), ((), ()))


def _rms(x, g):
    ms = jnp.mean(x * x, axis=-1, keepdims=True)
    return x * lax.rsqrt(ms + EPS) * g


def _silu(x):
    return x * jax.nn.sigmoid(x)


def _norm_block64(blk, g):
    lo = lax.broadcasted_iota(jnp.int32, (1, HEAD_W), 1) < MAP_W
    sq = blk * blk
    s_lo = jnp.sum(jnp.where(lo, sq, 0.0), axis=-1, keepdims=True)
    s_hi = jnp.sum(jnp.where(lo, 0.0, sq), axis=-1, keepdims=True)
    r = jnp.where(lo, lax.rsqrt(s_lo * (1.0 / MAP_W) + EPS), lax.rsqrt(s_hi * (1.0 / MAP_W) + EPS))
    return blk * r * g


def _norm_block128(blk, g):
    ms = jnp.mean(blk * blk, axis=-1, keepdims=True)
    return blk * lax.rsqrt(ms + EPS) * g


def _lam(lq_ref, lk_ref):
    e = jnp.exp(jnp.sum(lq_ref[...] * lk_ref[...], axis=-1, keepdims=True))
    return e[0:1, :] - e[1:2, :] + LAM_INIT


def _shifted_bias(dist, rel_ref, h):
    n = jnp.maximum(dist, 0)
    nf = jnp.maximum(n, 1).astype(F32)
    large = 16 + (jnp.log(nf / 16) / math.log(128 / 16) * 16).astype(jnp.int32)
    large = jnp.minimum(large, NUM_BUCKETS - 1)
    bucket = jnp.where(n < 16, n, large)
    far = rel_ref[(NUM_BUCKETS - 1) * HEADS + h]
    out = jnp.zeros(dist.shape, F32)
    for b in range(NUM_BUCKETS - 1):
        out = jnp.where(bucket == b, rel_ref[b * HEADS + h] - far, out)
    return out


def _project_segments(h, w_ref, gq_ref, gk_ref, gcq_ref, k_ref, v_ref, qb_ref, kb_ref, vb_ref,
                      cqb_ref, sgb_ref, sgc_ref, q_scale, v_head_rows):
    rows = h.shape[0]

    def seg(s):
        return jnp.dot(h, w_ref[:, s * SEG:(s + 1) * SEG], preferred_element_type=F32)

    q = seg(2)
    for c in range(HEADS):
        sl = slice(c * HEAD_W, (c + 1) * HEAD_W)
        qb_ref[:, sl] = (_norm_block64(q[:, sl], gq_ref[:, sl]) * q_scale).astype(qb_ref.dtype)
    k = seg(3)
    for c in range(HEADS):
        sl = slice(c * HEAD_W, (c + 1) * HEAD_W)
        kn = _norm_block64(k[:, sl], gk_ref[:, sl])
        k_ref[:, sl] = kn
        if kb_ref is not None:
            kb_ref[:, sl] = kn.astype(BF16)
    v = seg(4)
    if v_head_rows:
        for c in range(HEADS):
            v_ref[pl.ds(c, rows, stride=HEADS), :] = v[:, c * HEAD_W:(c + 1) * HEAD_W]
    else:
        v_ref[...] = v
    if vb_ref is not None:
        vb_ref[...] = v.astype(BF16)
    sgb_ref[...] = _silu(seg(5)).astype(sgb_ref.dtype)
    cq = seg(6)
    for c in range(HEADS):
        sl = slice(c * HEAD_W, (c + 1) * HEAD_W)
        cqb_ref[:, sl] = _norm_block128(cq[:, sl], gcq_ref[:, sl]).astype(cqb_ref.dtype)
    sgc_ref[...] = _silu(seg(7)).astype(sgc_ref.dtype)


def _inproj_prompt_kernel(x_ref, ng_ref, w_ref, gq_ref, gk_ref, gcq_ref, pw_ref, ps_ref,
                          k_ref, v_ref, pool_ref, qb_ref, kb_ref, vb_ref, cqb_ref, sgb_ref, sgc_ref,
                          mixa_ref, uext_ref):
    tm = ROW_TILE
    i = pl.program_id(1)

    @pl.when(i == 0)
    def _():
        uext_ref[0:16, :] = jnp.zeros((16, SEG), F32)

    h = _rms(x_ref[...], ng_ref[...]).astype(BF16)
    u = jnp.dot(h, w_ref[:, 0:SEG], preferred_element_type=F32)
    uext_ref[16:16 + tm, :] = u
    sga = _silu(jnp.dot(h, w_ref[:, SEG:2 * SEG], preferred_element_type=F32))
    pos = i * tm + lax.broadcasted_iota(jnp.int32, (tm, 1), 0)
    for g, win in enumerate(POOL_WINDOWS):
        sl = slice(g * HEAD_W, (g + 1) * HEAD_W)
        ug = u[:, sl]
        acc = ug
        for j in range(1, win):
            acc = acc + uext_ref[16 - j:16 - j + tm, sl]
        cnt = jnp.minimum(pos + 1, win).astype(F32)
        d = acc / cnt - ug
        y = jnp.dot(d.astype(BF16), pw_ref[g], preferred_element_type=F32) * ps_ref[:, sl]
        mixa_ref[:, sl] = (y * sga[:, sl]).astype(BF16)

    @pl.when(i == pl.num_programs(1) - 1)
    def _():
        pool_ref[...] = uext_ref[tm + 1:tm + 16, :]

    uext_ref[0:16, :] = uext_ref[tm:tm + 16, :]

    _project_segments(h, w_ref, gq_ref, gk_ref, gcq_ref, k_ref, v_ref, qb_ref, kb_ref, vb_ref,
                      cqb_ref, sgb_ref, sgc_ref, q_scale=DIFF_SCALE * LOG2E, v_head_rows=True)


def _inproj_prompt(x, ng, w_in, gq, gk, gcq, pw, ps):
    b, s, _ = x.shape
    tm = ROW_TILE
    row = lambda width: pl.BlockSpec((None, tm, width), lambda bi, i: (bi, i, 0))
    const = lambda shape: pl.BlockSpec(shape, lambda bi, i: (0,) * len(shape))
    out_shape = (
        jax.ShapeDtypeStruct((b, s, SEG), F32),
        jax.ShapeDtypeStruct((b, s * HEADS, HEAD_W), F32),
        jax.ShapeDtypeStruct((b, POOL_PAD, SEG), F32),
        jax.ShapeDtypeStruct((b, s, SEG), BF16),
        jax.ShapeDtypeStruct((b, s, SEG), BF16),
        jax.ShapeDtypeStruct((b, s, SEG), BF16),
        jax.ShapeDtypeStruct((b, s, SEG), BF16),
        jax.ShapeDtypeStruct((b, s, SEG), BF16),
        jax.ShapeDtypeStruct((b, s, SEG), BF16),
        jax.ShapeDtypeStruct((b, s, SEG), BF16),
    )
    out_specs = (row(SEG), pl.BlockSpec((None, tm * HEADS, HEAD_W), lambda bi, i: (bi, i, 0)),
                 pl.BlockSpec((None, POOL_PAD, SEG), lambda bi, i: (bi, 0, 0)),
                 row(SEG), row(SEG), row(SEG), row(SEG), row(SEG), row(SEG), row(SEG))
    return pl.pallas_call(
        _inproj_prompt_kernel,
        out_shape=out_shape,
        grid=(b, s // tm),
        in_specs=[row(D_MODEL), const((1, D_MODEL)),
                  pl.BlockSpec((D_MODEL, N_SEG * SEG), lambda bi, i: (0, 0), pipeline_mode=pl.Buffered(1)),
                  const((1, SEG)), const((1, SEG)), const((1, SEG)),
                  const((len(POOL_WINDOWS), HEAD_W, HEAD_W)), const((1, SEG))],
        out_specs=out_specs,
        scratch_shapes=[pltpu.VMEM((tm + 16, SEG), F32)],
        compiler_params=pltpu.CompilerParams(dimension_semantics=("arbitrary", "arbitrary"),
                                             vmem_limit_bytes=VMEM_LIMIT),
        name="inproj_prompt",
    )(x, ng, w_in, gq, gk, gcq, pw, ps)


def _memkv_kernel(mem_ref, mg_ref, w_ref, gck_ref, mk_ref, mv_ref, mkb_ref, mvb_ref):
    h = _rms(mem_ref[...], mg_ref[...]).astype(BF16)
    k = jnp.dot(h, w_ref[:, 0:SEG], preferred_element_type=F32)
    for c in range(HEADS):
        sl = slice(c * HEAD_W, (c + 1) * HEAD_W)
        kn = _norm_block128(k[:, sl], gck_ref[:, sl])
        mk_ref[:, sl] = kn
        mkb_ref[:, sl] = kn.astype(BF16)
    v = jnp.dot(h, w_ref[:, SEG:2 * SEG], preferred_element_type=F32)
    mv_ref[...] = v
    mvb_ref[...] = v.astype(BF16)


def _memkv(mem, mg, w_kv, gck):
    b = mem.shape[0]
    blk = lambda width: pl.BlockSpec((None, N_MEM, width), lambda bi: (bi, 0, 0))
    const = lambda shape: pl.BlockSpec(shape, lambda bi: (0,) * len(shape))
    return pl.pallas_call(
        _memkv_kernel,
        out_shape=(jax.ShapeDtypeStruct((b, N_MEM, SEG), F32), jax.ShapeDtypeStruct((b, N_MEM, SEG), F32),
                   jax.ShapeDtypeStruct((b, N_MEM, SEG), BF16), jax.ShapeDtypeStruct((b, N_MEM, SEG), BF16)),
        grid=(b,),
        in_specs=[blk(D_MODEL), const((1, D_MODEL)), const((D_MODEL, 2 * SEG)), const((1, SEG))],
        out_specs=(blk(SEG), blk(SEG), blk(SEG), blk(SEG)),
        compiler_params=pltpu.CompilerParams(dimension_semantics=("arbitrary",),
                                             vmem_limit_bytes=VMEM_LIMIT),
        name="mem_kv",
    )(mem, mg, w_kv, gck)


def _attn_kernel(pt_ref, rel_ref, x_ref, qb_ref, kb_ref, vb_ref, cqb_ref, sgb_ref, sgc_ref, mixa_ref,
                 mkb_ref, mvb_ref, wout_ref, hg_ref, lq_ref, lk_ref,
                 sin_hbm, memk_hbm, memv_hbm, kt_hbm, v_hbm,
                 o_ref, sout_hbm,
                 bdiag_ref, bsub_ref, qq_ref, m_ref, l_ref, acc_ref, mix_ref,
                 kbuf, vbuf, ring_sem, sin_buf, mk_buf, mv_buf, fetch_sem, stage, out_sem,
                 qbd_ref, sm_ref, sl_ref, sacc_ref, blast_ref, bnew_ref, kpad_ref, vpad_ref,
                 *, n_seq, n_mem_seq, n_new):
    t = ATT_TILE
    npg = PAGES_PER_TILE
    r8 = SAMPLE_ROWS
    hr = 2 * r8
    groups_per_seq = pt_ref.shape[1] // npg
    assert groups_per_seq & (groups_per_seq - 1) == 0
    seq_shift = groups_per_seq.bit_length() - 1
    bi = pl.program_id(0)
    qi = pl.program_id(1)
    nqb = kb_ref.shape[0] // t
    first_step = (bi == 0) & (qi == 0)
    last_step = (bi == pl.num_programs(0) - 1) & (qi == pl.num_programs(1) - 1)
    u_base = bi * (nqb * (nqb + 1) // 2) + jnp.right_shift(qi * (qi + 1), 1)

    lane = lax.broadcasted_iota(jnp.int32, (1, SEG), 1)

    def static(v):
        return isinstance(v, int)

    def ring_slot(pos):
        return pos % RING if static(pos) else lax.rem(pos, jnp.int32(RING))

    def clamp(v, hi):
        return min(v, hi) if static(v) else jnp.minimum(v, hi)

    def ring_copies(group, slot):
        out = []
        for i in range(npg):
            if group is None:
                pid = 0
            else:
                pid = pt_ref[jnp.right_shift(group, seq_shift), (group & (groups_per_seq - 1)) * npg + i]
            out.append(pltpu.make_async_copy(kt_hbm.at[pid], kbuf.at[slot, i], ring_sem.at[slot]))
            out.append(pltpu.make_async_copy(v_hbm.at[pid], vbuf.at[slot, i], ring_sem.at[slot]))
        return out

    n_real_groups = n_mem_seq * groups_per_seq

    def ring_start(pos):
        def go():
            for n, cp in enumerate(ring_copies(pos, ring_slot(pos))):
                cp.start(priority=n % 2)
        if static(pos):
            assert pos < n_real_groups
            go()
        else:
            pl.when(pos < n_real_groups)(go)

    def ring_wait(pos):
        def go():
            for cp in ring_copies(None, ring_slot(pos)):
                cp.wait()
        pl.when(pos < n_real_groups)(go)

    def fetch_copies(seq):
        slot = seq & 1
        s_in = clamp(seq, n_seq - 1)
        s_mem = clamp(seq, n_mem_seq - 1)
        return (pltpu.make_async_copy(sin_hbm.at[s_in], sin_buf.at[slot], fetch_sem.at[slot]),
                pltpu.make_async_copy(memk_hbm.at[s_mem], mk_buf.at[slot], fetch_sem.at[slot]),
                pltpu.make_async_copy(memv_hbm.at[s_mem], mv_buf.at[slot], fetch_sem.at[slot]))

    def out_copy(row, slot):
        return pltpu.make_async_copy(stage.at[slot], sout_hbm.at[row], out_sem.at[slot])

    def sample_begin(slot):
        q8 = sin_buf[slot, 0:r8, :].astype(BF16)
        zero = jnp.zeros_like(q8)
        for hm in range(2 * HEADS):
            qbd_ref[hm * r8:(hm + 1) * r8, :] = jnp.where(
                (lane >= hm * MAP_W) & (lane < (hm + 1) * MAP_W), q8, zero)
        sm_ref[...] = jnp.full(sm_ref.shape, NEG_INF, F32)
        sl_ref[...] = jnp.zeros(sl_ref.shape, F32)
        sacc_ref[...] = jnp.zeros(sacc_ref.shape, F32)

    def sample_attend(s_parts, v_of):
        n = len(s_parts)
        s = jnp.concatenate(s_parts, axis=1) if n > 1 else s_parts[0]
        m_old = sm_ref[...]
        m_new = jnp.maximum(m_old, jnp.max(s, axis=-1, keepdims=True))
        alpha = jnp.exp(m_old - m_new)
        p = jnp.exp(s - (jnp.concatenate([m_new] * n, axis=1) if n > 1 else m_new))
        psum = p[:, 0:PAGE]
        for i in range(1, n):
            psum = psum + p[:, i * PAGE:(i + 1) * PAGE]
        sl_ref[...] = alpha * sl_ref[...] + psum
        pb = p.astype(BF16)
        heads = []
        for h in range(HEADS):
            pv = None
            for i in range(n):
                part = jnp.dot(pb[h * hr:(h + 1) * hr, i * PAGE:(i + 1) * PAGE], v_of(i, h),
                               preferred_element_type=F32)
                pv = part if pv is None else pv + part
            heads.append(pv)
        sacc_ref[...] = alpha * sacc_ref[...] + jnp.concatenate(heads, axis=0)
        sm_ref[...] = m_new

    def sample_scores(u):
        slot = ring_slot(u)
        is_last_group = (u & (groups_per_seq - 1)) == groups_per_seq - 1
        last_bias = jnp.where(is_last_group, blast_ref[...], 0.0)
        qbd = qbd_ref[...]
        s_parts = []
        for i in range(npg):
            s = jnp.dot(qbd, kbuf[slot, i].astype(BF16), preferred_element_type=F32)
            if i == npg - 1:
                s = s + last_bias
            s_parts.append(s)
        return s_parts

    def sample_values(u, s_parts):
        slots = [ring_slot(u + g) for g in range(len(s_parts) // npg)]
        sample_attend(s_parts, lambda i, h: vbuf[slots[i // npg], i % npg,
                                                 pl.ds(h, PAGE, stride=HEADS), :].astype(BF16))

    def sample_end(u):
        seq = jnp.right_shift(u, seq_shift)
        slot = seq & 1
        out_copy(0, slot).wait()

        cq8 = sin_buf[slot, r8:2 * r8, :].astype(BF16)
        cross_scores = []
        for h in range(HEADS):
            mk_h = mk_buf[slot, pl.ds(h, N_MEM, stride=HEADS), :].astype(BF16)
            cross_scores.append(lax.dot_general(cq8[:, h * HEAD_W:(h + 1) * HEAD_W], mk_h, _NT,
                                                preferred_element_type=F32) * CROSS_SCALE)
        kpad_ref[0:r8, :] = sin_buf[slot, 4 * r8:5 * r8, :].astype(BF16)
        vpad_ref[0:r8, :] = sin_buf[slot, 5 * r8:6 * r8, :].astype(BF16)
        s_new = lax.dot_general(qbd_ref[...], kpad_ref[...], _NT, preferred_element_type=F32) + bnew_ref[...]
        sample_attend([s_new], lambda i, h: vpad_ref[:, h * HEAD_W:(h + 1) * HEAD_W])

        cross_p = [jnp.exp(sc - jnp.max(sc, axis=-1, keepdims=True)) for sc in cross_scores]
        lam = _lam(lq_ref, lk_ref)
        o = sacc_ref[...] / jnp.sum(sl_ref[...], axis=-1, keepdims=True)
        for h in range(HEADS):
            sl = slice(h * HEAD_W, (h + 1) * HEAD_W)
            mv_h = mv_buf[slot, pl.ds(h, N_MEM, stride=HEADS), :].astype(BF16)
            oc = jnp.dot(cross_p[h].astype(BF16), mv_h, preferred_element_type=F32)
            oc = oc / jnp.sum(cross_p[h], axis=-1, keepdims=True)
            stage[slot, r8:2 * r8, sl] = oc * sin_buf[slot, 3 * r8:4 * r8, sl]
            a = o[h * hr:h * hr + r8, :] - lam * o[h * hr + r8:(h + 1) * hr, :]
            od = _norm_block128(a, hg_ref[:, sl]) * (1.0 - LAM_INIT)
            stage[slot, 0:r8, sl] = od * sin_buf[slot, 2 * r8:3 * r8, sl]
        out_copy(seq, slot).start()
        for cp in fetch_copies(seq + 1):
            cp.wait()
        sample_begin(1 - slot)
        for cp in fetch_copies(seq + 2):
            cp.start()

    @pl.when(first_step)
    def _():
        for g in range(LOOKAHEAD):
            ring_start(g)
        for cp in fetch_copies(0):
            cp.start()
        d = lax.broadcasted_iota(jnp.int32, (t, t), 0) - lax.broadcasted_iota(jnp.int32, (t, t), 1)
        for h in range(HEADS):
            bdiag_ref[h] = jnp.where(d >= 0, _shifted_bias(d, rel_ref, h) * LOG2E, NEG_INF)
            bsub_ref[h] = _shifted_bias(d + t, rel_ref, h) * LOG2E
        tok = lax.broadcasted_iota(jnp.int32, (hr, PAGE), 0) & (r8 - 1)
        key = lax.broadcasted_iota(jnp.int32, (hr, PAGE), 1)
        for h in range(HEADS):
            blast_ref[h * hr:(h + 1) * hr, :] = _shifted_bias(PAGE + tok - key, rel_ref, h)
            ok = (key <= tok) & (key < n_new)
            bnew_ref[h * hr:(h + 1) * hr, :] = jnp.where(ok, _shifted_bias(tok - key, rel_ref, h), NEG_INF)
        kpad_ref[...] = jnp.zeros(kpad_ref.shape, BF16)
        vpad_ref[...] = jnp.zeros(vpad_ref.shape, BF16)
        stage[...] = jnp.zeros(stage.shape, F32)
        out_copy(n_seq, 0).start()
        out_copy(n_seq + 1, 1).start()
        for cp in fetch_copies(0):
            cp.wait()
        sample_begin(0)
        for cp in fetch_copies(1):
            cp.start()

    lam = _lam(lq_ref, lk_ref)
    lo = lax.broadcasted_iota(jnp.int32, (1, HEAD_W), 1) < MAP_W
    mix_ref[:, 0:SEG] = mixa_ref[...]

    for h in range(HEADS):
        qh = qb_ref[:, h * HEAD_W:(h + 1) * HEAD_W]
        zero = jnp.zeros_like(qh)
        qq_ref[h, 0:t, :] = jnp.where(lo, qh, zero)
        qq_ref[h, t:2 * t, :] = jnp.where(lo, zero, qh)
    m_ref[...] = jnp.full(m_ref.shape, NEG_INF, F32)
    l_ref[...] = jnp.zeros(l_ref.shape, F32)
    acc_ref[...] = jnp.zeros(acc_ref.shape, F32)

    def tile_step(j0, bias_ref, u, n):
        w = n * t
        for g in range(n):
            ring_start(u + LOOKAHEAD + g)
        for g in range(n):
            ring_wait(u + g)
        s_parts = []
        for g in range(n):
            s_parts += sample_scores(u + g)
        def head_scores(h):
            s = lax.dot_general(qq_ref[h], kb_ref[pl.ds(j0, w), h * HEAD_W:(h + 1) * HEAD_W], _NT,
                                preferred_element_type=F32)
            if bias_ref is not None:
                bias = bias_ref[h]
                s = s + jnp.concatenate([bias, bias], axis=0)
            return s

        scores = [head_scores(h) for h in range(HEADS)] if n > 1 else None
        for h in range(HEADS):
            sl = slice(h * HEAD_W, (h + 1) * HEAD_W)
            s = scores[h] if n > 1 else head_scores(h)
            m_old = m_ref[h]
            m_new = jnp.maximum(m_old, jnp.max(s, axis=-1, keepdims=True))
            alpha = jnp.exp2(m_old - m_new)
            p = jnp.exp2(s - jnp.concatenate([m_new] * (w // HEAD_W), axis=1))
            psum = p[:, 0:HEAD_W]
            for c in range(1, w // HEAD_W):
                psum = psum + p[:, c * HEAD_W:(c + 1) * HEAD_W]
            l_ref[h] = alpha * l_ref[h] + psum
            acc_ref[h] = alpha * acc_ref[h] + jnp.dot(p.astype(BF16), vb_ref[pl.ds(j0, w), sl],
                                                      preferred_element_type=F32)
            m_ref[h] = m_new
        sample_values(u, s_parts)

        u_last = u + n - 1

        @pl.when((u_last & (groups_per_seq - 1)) == groups_per_seq - 1)
        def _():
            sample_end(u_last)

    tile_step(pl.multiple_of(qi * t, t), bdiag_ref, u_base, 1)

    @pl.when(qi >= 1)
    def _():
        tile_step(pl.multiple_of((qi - 1) * t, t), bsub_ref, u_base + 1, 1)

    n_far = jnp.maximum(qi - 1, 0)
    u_far = u_base + 2
    lead = jnp.where((n_far >= 1) & ((u_far & 1) == 1), 1, 0)
    n_pairs = jnp.right_shift(n_far - lead, 1)
    trail = (n_far - lead) & 1

    @pl.when(lead == 1)
    def _():
        tile_step(0, None, u_far, 1)

    def far_pair(k, carry):
        j = lead + 2 * k
        tile_step(pl.multiple_of(j * t, t), None, u_far + j, 2)
        return carry

    lax.fori_loop(0, n_pairs, far_pair, 0)

    @pl.when(trail == 1)
    def _():
        tile_step(pl.multiple_of((n_far - 1) * t, t), None, u_far + n_far - 1, 1)

    cross_scores = [lax.dot_general(cqb_ref[:, h * HEAD_W:(h + 1) * HEAD_W], mkb_ref[:, h * HEAD_W:(h + 1) * HEAD_W],
                                    _NT, preferred_element_type=F32) * CROSS_SCALE for h in range(HEADS)]
    for h in range(HEADS):
        sl = slice(h * HEAD_W, (h + 1) * HEAD_W)
        o = acc_ref[h] / jnp.sum(l_ref[h], axis=-1, keepdims=True)
        a = o[0:t, :] - lam * o[t:2 * t, :]
        od = _norm_block128(a, hg_ref[:, sl]) * (1.0 - LAM_INIT)
        mix_ref[:, SEG + h * HEAD_W:SEG + (h + 1) * HEAD_W] = (od * sgb_ref[:, sl].astype(F32)).astype(BF16)

        sc = cross_scores[h]
        pc = jnp.exp(sc - jnp.max(sc, axis=-1, keepdims=True))
        c = jnp.dot(pc.astype(BF16), mvb_ref[:, sl], preferred_element_type=F32)
        c = c / jnp.sum(pc, axis=-1, keepdims=True)
        mix_ref[:, 2 * SEG + h * HEAD_W:2 * SEG + (h + 1) * HEAD_W] = (c * sgc_ref[:, sl].astype(F32)).astype(BF16)

    o_ref[...] = x_ref[...] + jnp.dot(mix_ref[...], wout_ref[...], preferred_element_type=F32)

    @pl.when(last_step)
    def _():
        for cp in fetch_copies(n_seq + 1):
            cp.wait()
        out_copy(0, 0).wait()
        out_copy(0, 1).wait()


def _attn(page_table, rel_flat, x, qb, kb, vb, cqb, sgb, sgc, mixa, mkb, mvb, w_out, hg, lq, lk,
          sin, mem_k3, mem_v3, cache_kt, cache_v3, n_new):
    b, s, _ = x.shape
    t = ATT_TILE
    nqb = s // t
    n_tile_steps = b * nqb * (nqb + 1) // 2
    n_seq = sin.shape[0]
    assert n_tile_steps * PAGES_PER_TILE == n_seq * page_table.shape[1]
    assert sin.shape == (n_seq, SIN_ROWS, SEG) and page_table.shape[0] == mem_k3.shape[0]
    row = lambda width: pl.BlockSpec((None, t, width), lambda bi, i, pt: (bi, i, 0))
    seq = lambda n: pl.BlockSpec((None, n, SEG), lambda bi, i, pt: (bi, 0, 0), pipeline_mode=pl.Buffered(1))
    const = lambda shape: pl.BlockSpec(shape, lambda bi, i, pt: (0,) * len(shape))
    hbm = pl.BlockSpec(memory_space=pl.ANY)
    r8 = SAMPLE_ROWS
    nq = HEADS * 2 * r8
    grid_spec = pltpu.PrefetchScalarGridSpec(
        num_scalar_prefetch=1,
        grid=(b, nqb),
        in_specs=[pl.BlockSpec(memory_space=pltpu.SMEM),
                  row(D_MODEL), row(SEG), seq(s), seq(s), row(SEG), row(SEG), row(SEG), row(SEG),
                  seq(N_MEM), seq(N_MEM), const((3 * SEG, D_MODEL)), const((1, SEG)),
                  const((2, MAP_W)), const((2, MAP_W)), hbm, hbm, hbm, hbm, hbm],
        out_specs=(row(D_MODEL), hbm),
        scratch_shapes=[
            pltpu.VMEM((HEADS, t, t), F32), pltpu.VMEM((HEADS, t, t), F32),
            pltpu.VMEM((HEADS, 2 * t, HEAD_W), BF16),
            pltpu.VMEM((HEADS, 2 * t, HEAD_W), F32), pltpu.VMEM((HEADS, 2 * t, HEAD_W), F32),
            pltpu.VMEM((HEADS, 2 * t, HEAD_W), F32), pltpu.VMEM((t, 3 * SEG), BF16),
            pltpu.VMEM((RING, PAGES_PER_TILE, SEG, PAGE), F32),
            pltpu.VMEM((RING, PAGES_PER_TILE, PAGE * HEADS, HEAD_W), F32),
            pltpu.SemaphoreType.DMA((RING,)),
            pltpu.VMEM((2, SIN_ROWS, SEG), F32), pltpu.VMEM((2, N_MEM * HEADS, HEAD_W), F32),
            pltpu.VMEM((2, N_MEM * HEADS, HEAD_W), F32), pltpu.SemaphoreType.DMA((2,)),
            pltpu.VMEM((2, 2 * r8, SEG), F32), pltpu.SemaphoreType.DMA((2,)),
            pltpu.VMEM((nq, SEG), BF16),
            pltpu.VMEM((nq, HEAD_W), F32), pltpu.VMEM((nq, HEAD_W), F32), pltpu.VMEM((nq, HEAD_W), F32),
            pltpu.VMEM((nq, PAGE), F32), pltpu.VMEM((nq, PAGE), F32),
            pltpu.VMEM((PAGE, SEG), BF16), pltpu.VMEM((PAGE, SEG), BF16)],
    )
    kern = functools.partial(_attn_kernel, n_seq=n_seq, n_mem_seq=mem_k3.shape[0], n_new=n_new)
    return pl.pallas_call(
        kern,
        out_shape=(jax.ShapeDtypeStruct((b, s, D_MODEL), F32),
                   jax.ShapeDtypeStruct((n_seq + 2, 2 * r8, SEG), F32)),
        grid_spec=grid_spec,
        compiler_params=pltpu.CompilerParams(dimension_semantics=("arbitrary", "arbitrary"),
                                             vmem_limit_bytes=VMEM_LIMIT_ATTN),
        name="attn",
    )(page_table, rel_flat, x, qb, kb, vb, cqb, sgb, sgc, mixa, mkb, mvb, w_out, hg, lq, lk,
      sin, mem_k3, mem_v3, cache_kt, cache_v3)


def _inproj_sample_kernel(x_ref, st_ref, ng_ref, w_ref, gq_ref, gk_ref, gcq_ref, pw_ref, ps_ref,
                          k_ref, v_ref, pool_ref, q_ref, cq_ref, sgb_ref, sgc_ref, mixa_ref, d_ref):
    nb = st_ref.shape[1]
    nt = x_ref.shape[0] // nb
    h = _rms(x_ref[...], ng_ref[...]).astype(BF16)
    u = jnp.dot(h, w_ref[:, 0:SEG], preferred_element_type=F32)
    sga = _silu(jnp.dot(h, w_ref[:, SEG:2 * SEG], preferred_element_type=F32))

    def ext(e, sl):
        if e < POOL_PAD:
            return st_ref[e, :, sl]
        return u[(e - POOL_PAD) * nb:(e - POOL_PAD + 1) * nb, sl]

    for g, win in enumerate(POOL_WINDOWS):
        sl = slice(g * HEAD_W, (g + 1) * HEAD_W)
        for tk in range(nt):
            acc = ext(POOL_PAD + tk, sl)
            for j in range(1, win):
                acc = acc + ext(POOL_PAD + tk - j, sl)
            d_ref[tk * nb:(tk + 1) * nb, sl] = acc / float(win) - ext(POOL_PAD + tk, sl)
        y = jnp.dot(d_ref[:, sl].astype(BF16), pw_ref[g], preferred_element_type=F32) * ps_ref[:, sl]
        mixa_ref[:, sl] = y * sga[:, sl]

    keep = POOL_PAD - nt
    pool_ref[0:keep] = st_ref[nt:POOL_PAD]
    for tk in range(nt):
        pool_ref[keep + tk] = u[tk * nb:(tk + 1) * nb, :]

    _project_segments(h, w_ref, gq_ref, gk_ref, gcq_ref, k_ref, v_ref, q_ref, None, None,
                      cq_ref, sgb_ref, sgc_ref, q_scale=DIFF_SCALE, v_head_rows=False)


def _inproj_sample(x_tm, state_tm, ng, w_in, gq, gk, gcq, pw, ps):
    rows = x_tm.shape[0]
    nb = state_tm.shape[1]
    vmem = pl.BlockSpec(memory_space=pltpu.VMEM)
    mat = jax.ShapeDtypeStruct((rows, SEG), F32)
    return pl.pallas_call(
        _inproj_sample_kernel,
        out_shape=(mat, mat, jax.ShapeDtypeStruct((POOL_PAD, nb, SEG), F32), mat, mat, mat, mat, mat),
        in_specs=[vmem] * 9,
        out_specs=(vmem,) * 8,
        scratch_shapes=[pltpu.VMEM((rows, SEG), F32)],
        compiler_params=pltpu.CompilerParams(vmem_limit_bytes=VMEM_LIMIT),
        name="inproj_sample",
    )(x_tm, state_tm, ng, w_in, gq, gk, gcq, pw, ps)


def _outproj_kernel(x_ref, a_ref, d_ref, c_ref, w_ref, o_ref):
    y = jnp.dot(a_ref[...].astype(BF16), w_ref[0:SEG, :], preferred_element_type=F32)
    y = y + jnp.dot(d_ref[...].astype(BF16), w_ref[SEG:2 * SEG, :], preferred_element_type=F32)
    y = y + jnp.dot(c_ref[...].astype(BF16), w_ref[2 * SEG:3 * SEG, :], preferred_element_type=F32)
    o_ref[...] = x_ref[...] + y


def _outproj(x, a, d, c, w_out):
    rows = x.shape[0]
    vmem = pl.BlockSpec(memory_space=pltpu.VMEM)
    return pl.pallas_call(
        _outproj_kernel,
        out_shape=jax.ShapeDtypeStruct((rows, D_MODEL), F32),
        in_specs=[vmem] * 5,
        out_specs=vmem,
        compiler_params=pltpu.CompilerParams(vmem_limit_bytes=VMEM_LIMIT),
        name="outproj_sample",
    )(x, a, d, c, w_out)


def _tile_gain(g, reps):
    return jnp.tile(g.reshape(1, -1), (1, reps))


def kernel(x_prompt, x_sample, mem_prompt, cache_k, cache_v, cache_mem_k, cache_mem_v, state_pool, page_table,
           norm_g, w_in, q_norm_g, k_norm_g, cq_norm_g, ck_norm_g, mem_norm_g, w_mem_kv, lam_q, lam_k,
           head_norm_g, pool_w, pool_scale, rel_bias, w_out):
    depth = w_in.shape[0]
    assert depth == 1
    l = 0
    b, s, _ = x_prompt.shape
    nb, nt, _ = x_sample.shape
    r8 = SAMPLE_ROWS

    ng = norm_g[l].reshape(1, D_MODEL)
    w_in_b = w_in[l].astype(BF16)
    gq = _tile_gain(q_norm_g[l], SEG // MAP_W)
    gk = _tile_gain(k_norm_g[l], SEG // MAP_W)
    gcq = _tile_gain(cq_norm_g[l], HEADS)
    gck = _tile_gain(ck_norm_g[l], HEADS)
    pw = pool_w[l].astype(BF16)
    ps = pool_scale[l].reshape(1, SEG)
    hg = head_norm_g[l].reshape(1, SEG)
    w_out_b = w_out[l].astype(BF16)
    rel_flat = rel_bias.reshape(-1)

    k_p, v_p, pool_p, qb, kb, vb, cqb, sgb, sgc, mixa = _inproj_prompt(x_prompt, ng, w_in_b, gq, gk, gcq, pw, ps)
    mk, mv, mkb, mvb = _memkv(mem_prompt, mem_norm_g[l].reshape(1, D_MODEL), w_mem_kv[l].astype(BF16), gck)
    x_tm = x_sample.transpose(1, 0, 2).reshape(nt * nb, D_MODEL)
    st_tm = state_pool[l].transpose(1, 0, 2)
    k_s, v_s, pool_s, q_s, cq_s, sgb_s, sgc_s, mixa_s = _inproj_sample(x_tm, st_tm, ng, w_in_b, gq, gk, gcq, pw, ps)

    def seq_major(a, rows):
        a = a.reshape(nt, nb, SEG).transpose(1, 0, 2)
        return a if rows == nt else jnp.pad(a, ((0, 0), (0, rows - nt), (0, 0)))

    kn = seq_major(k_s, nt)
    vn = seq_major(v_s, nt)

    nqb = s // ATT_TILE
    n_tile_steps = b * nqb * (nqb + 1) // 2
    n_pages = page_table.shape[1]
    n_seq = n_tile_steps * PAGES_PER_TILE // n_pages
    assert n_seq >= nb and n_seq * n_pages == n_tile_steps * PAGES_PER_TILE
    sin = jnp.concatenate([seq_major(q_s, r8), seq_major(cq_s, r8), seq_major(sgb_s, r8), seq_major(sgc_s, r8),
                           jnp.pad(kn, ((0, 0), (0, r8 - nt), (0, 0))), jnp.pad(vn, ((0, 0), (0, r8 - nt), (0, 0)))],
                          axis=1)
    sin = jnp.pad(sin, ((0, n_seq - nb), (0, 0), (0, 0)))
    n_pool = cache_k.shape[1]
    cache_kt = jnp.transpose(cache_k[l], (0, 2, 3, 4, 1)).reshape(n_pool, SEG, PAGE)
    cache_v3 = cache_v[l].reshape(n_pool, PAGE * HEADS, HEAD_W)
    mem_k3 = cache_mem_k[l].reshape(nb, N_MEM * HEADS, HEAD_W)
    mem_v3 = cache_mem_v[l].reshape(nb, N_MEM * HEADS, HEAD_W)
    y_p, sres = _attn(page_table, rel_flat, x_prompt, qb, kb, vb, cqb, sgb, sgc, mixa, mkb, mvb, w_out_b, hg,
                      lam_q[l], lam_k[l], sin, mem_k3, mem_v3, cache_kt, cache_v3, nt)
    mixd = sres[:nb, 0:r8]
    mixc = sres[:nb, r8:2 * r8]

    x8 = jnp.pad(x_sample, ((0, 0), (0, r8 - nt), (0, 0))).reshape(nb * r8, D_MODEL)
    y8 = _outproj(x8, seq_major(mixa_s, r8).reshape(nb * r8, SEG), mixd.reshape(nb * r8, SEG),
                  mixc.reshape(nb * r8, SEG), w_out_b)
    y_s = y8.reshape(nb, r8, D_MODEL)[:, :nt]

    return (y_p, y_s,
            k_p.reshape(1, b, s, HEADS, 2, MAP_W), v_p.reshape(1, b, s, HEADS, HEAD_W),
            pool_p[None], mk.reshape(1, b, N_MEM, HEADS, HEAD_W), mv.reshape(1, b, N_MEM, HEADS, HEAD_W),
            kn.reshape(1, nb, nt, HEADS, 2, MAP_W), vn.reshape(1, nb, nt, HEADS, HEAD_W),
            pool_s.transpose(1, 0, 2)[None])
```

```python
import functools
import math

import jax
import jax.numpy as jnp
from jax import lax
from jax.experimental import pallas as pl
from jax.experimental.pallas import tpu as pltpu

F32 = jnp.float32
BF16 = jnp.bfloat16

D_MODEL = 1024
SEG = 512
N_SEG = 8
POOL_WINDOWS = (2, 4, 8, 16)
POOL_PAD = 15
HEADS = 4
HEAD_W = 128
MAP_W = 64
N_MEM = 256
PAGE = 128
NUM_BUCKETS = 32
EPS = 1e-6
NEG_INF = -1e30
LAM_INIT = 0.8 - 0.6 * math.exp(-0.3 * 0)
DIFF_SCALE = MAP_W ** -0.5
CROSS_SCALE = HEAD_W ** -0.5
LOG2E = math.log2(math.e)

ROW_TILE = 1024
ATT_TILE = 256
SAMPLE_ROWS = 8
PAGES_PER_TILE = 8
LOOKAHEAD = 3
RING = LOOKAHEAD + 2
SIN_ROWS = 6 * SAMPLE_ROWS
VMEM_LIMIT = 48 * 1024 * 1024
VMEM_LIMIT_ATTN = 56 * 1024 * 1024

_NT = (((1,), (1,)), ((), ()))


def _rms(x, g):
    ms = jnp.mean(x * x, axis=-1, keepdims=True)
    return x * lax.rsqrt(ms + EPS) * g


def _silu(x):
    return x * jax.nn.sigmoid(x)


def _norm_block64(blk, g):
    lo = lax.broadcasted_iota(jnp.int32, (1, HEAD_W), 1) < MAP_W
    sq = blk * blk
    s_lo = jnp.sum(jnp.where(lo, sq, 0.0), axis=-1, keepdims=True)
    s_hi = jnp.sum(jnp.where(lo, 0.0, sq), axis=-1, keepdims=True)
    r = jnp.where(lo, lax.rsqrt(s_lo * (1.0 / MAP_W) + EPS), lax.rsqrt(s_hi * (1.0 / MAP_W) + EPS))
    return blk * r * g


def _norm_block128(blk, g):
    ms = jnp.mean(blk * blk, axis=-1, keepdims=True)
    return blk * lax.rsqrt(ms + EPS) * g


def _lam(lq_ref, lk_ref):
    e = jnp.exp(jnp.sum(lq_ref[...] * lk_ref[...], axis=-1, keepdims=True))
    return e[0:1, :] - e[1:2, :] + LAM_INIT


def _shifted_bias(dist, rel_ref, h):
    n = jnp.maximum(dist, 0)
    nf = jnp.maximum(n, 1).astype(F32)
    large = 16 + (jnp.log(nf / 16) / math.log(128 / 16) * 16).astype(jnp.int32)
    large = jnp.minimum(large, NUM_BUCKETS - 1)
    bucket = jnp.where(n < 16, n, large)
    far = rel_ref[(NUM_BUCKETS - 1) * HEADS + h]
    out = jnp.zeros(dist.shape, F32)
    for b in range(NUM_BUCKETS - 1):
        out = jnp.where(bucket == b, rel_ref[b * HEADS + h] - far, out)
    return out


def _project_segments(h, w_ref, gq_ref, gk_ref, gcq_ref, k_ref, v_ref, qb_ref, kb_ref, vb_ref,
                      cqb_ref, sgb_ref, sgc_ref, q_scale, v_head_rows):
    rows = h.shape[0]

    def seg(s):
        return jnp.dot(h, w_ref[:, s * SEG:(s + 1) * SEG], preferred_element_type=F32)

    q = seg(2)
    for c in range(HEADS):
        sl = slice(c * HEAD_W, (c + 1) * HEAD_W)
        qb_ref[:, sl] = (_norm_block64(q[:, sl], gq_ref[:, sl]) * q_scale).astype(qb_ref.dtype)
    k = seg(3)
    for c in range(HEADS):
        sl = slice(c * HEAD_W, (c + 1) * HEAD_W)
        kn = _norm_block64(k[:, sl], gk_ref[:, sl])
        k_ref[:, sl] = kn
        if kb_ref is not None:
            kb_ref[:, sl] = kn.astype(BF16)
    v = seg(4)
    if v_head_rows:
        for c in range(HEADS):
            v_ref[pl.ds(c, rows, stride=HEADS), :] = v[:, c * HEAD_W:(c + 1) * HEAD_W]
    else:
        v_ref[...] = v
    if vb_ref is not None:
        vb_ref[...] = v.astype(BF16)
    sgb_ref[...] = _silu(seg(5)).astype(sgb_ref.dtype)
    cq = seg(6)
    for c in range(HEADS):
        sl = slice(c * HEAD_W, (c + 1) * HEAD_W)
        cqb_ref[:, sl] = _norm_block128(cq[:, sl], gcq_ref[:, sl]).astype(cqb_ref.dtype)
    sgc_ref[...] = _silu(seg(7)).astype(sgc_ref.dtype)


def _inproj_prompt_kernel(x_ref, ng_ref, w_ref, gq_ref, gk_ref, gcq_ref, pw_ref, ps_ref,
                          k_ref, v_ref, pool_ref, qb_ref, kb_ref, vb_ref, cqb_ref, sgb_ref, sgc_ref,
                          mixa_ref, uext_ref):
    tm = ROW_TILE
    i = pl.program_id(1)

    @pl.when(i == 0)
    def _():
        uext_ref[0:16, :] = jnp.zeros((16, SEG), F32)

    h = _rms(x_ref[...], ng_ref[...]).astype(BF16)
    u = jnp.dot(h, w_ref[:, 0:SEG], preferred_element_type=F32)
    uext_ref[16:16 + tm, :] = u
    sga = _silu(jnp.dot(h, w_ref[:, SEG:2 * SEG], preferred_element_type=F32))
    pos = i * tm + lax.broadcasted_iota(jnp.int32, (tm, 1), 0)
    for g, win in enumerate(POOL_WINDOWS):
        sl = slice(g * HEAD_W, (g + 1) * HEAD_W)
        ug = u[:, sl]
        acc = ug
        for j in range(1, win):
            acc = acc + uext_ref[16 - j:16 - j + tm, sl]
        cnt = jnp.minimum(pos + 1, win).astype(F32)
        d = acc / cnt - ug
        y = jnp.dot(d.astype(BF16), pw_ref[g], preferred_element_type=F32) * ps_ref[:, sl]
        mixa_ref[:, sl] = (y * sga[:, sl]).astype(BF16)

    @pl.when(i == pl.num_programs(1) - 1)
    def _():
        pool_ref[...] = uext_ref[tm + 1:tm + 16, :]

    uext_ref[0:16, :] = uext_ref[tm:tm + 16, :]

    _project_segments(h, w_ref, gq_ref, gk_ref, gcq_ref, k_ref, v_ref, qb_ref, kb_ref, vb_ref,
                      cqb_ref, sgb_ref, sgc_ref, q_scale=DIFF_SCALE * LOG2E, v_head_rows=True)


def _inproj_prompt(x, ng, w_in, gq, gk, gcq, pw, ps):
    b, s, _ = x.shape
    tm = ROW_TILE
    row = lambda width: pl.BlockSpec((None, tm, width), lambda bi, i: (bi, i, 0))
    const = lambda shape: pl.BlockSpec(shape, lambda bi, i: (0,) * len(shape))
    out_shape = (
        jax.ShapeDtypeStruct((b, s, SEG), F32),
        jax.ShapeDtypeStruct((b, s * HEADS, HEAD_W), F32),
        jax.ShapeDtypeStruct((b, POOL_PAD, SEG), F32),
        jax.ShapeDtypeStruct((b, s, SEG), BF16),
        jax.ShapeDtypeStruct((b, s, SEG), BF16),
        jax.ShapeDtypeStruct((b, s, SEG), BF16),
        jax.ShapeDtypeStruct((b, s, SEG), BF16),
        jax.ShapeDtypeStruct((b, s, SEG), BF16),
        jax.ShapeDtypeStruct((b, s, SEG), BF16),
        jax.ShapeDtypeStruct((b, s, SEG), BF16),
    )
    out_specs = (row(SEG), pl.BlockSpec((None, tm * HEADS, HEAD_W), lambda bi, i: (bi, i, 0)),
                 pl.BlockSpec((None, POOL_PAD, SEG), lambda bi, i: (bi, 0, 0)),
                 row(SEG), row(SEG), row(SEG), row(SEG), row(SEG), row(SEG), row(SEG))
    return pl.pallas_call(
        _inproj_prompt_kernel,
        out_shape=out_shape,
        grid=(b, s // tm),
        in_specs=[row(D_MODEL), const((1, D_MODEL)),
                  pl.BlockSpec((D_MODEL, N_SEG * SEG), lambda bi, i: (0, 0), pipeline_mode=pl.Buffered(1)),
                  const((1, SEG)), const((1, SEG)), const((1, SEG)),
                  const((len(POOL_WINDOWS), HEAD_W, HEAD_W)), const((1, SEG))],
        out_specs=out_specs,
        scratch_shapes=[pltpu.VMEM((tm + 16, SEG), F32)],
        compiler_params=pltpu.CompilerParams(dimension_semantics=("arbitrary", "arbitrary"),
                                             vmem_limit_bytes=VMEM_LIMIT),
        name="inproj_prompt",
    )(x, ng, w_in, gq, gk, gcq, pw, ps)


def _memkv_kernel(mem_ref, mg_ref, w_ref, gck_ref, mk_ref, mv_ref, mkb_ref, mvb_ref):
    h = _rms(mem_ref[...], mg_ref[...]).astype(BF16)
    k = jnp.dot(h, w_ref[:, 0:SEG], preferred_element_type=F32)
    for c in range(HEADS):
        sl = slice(c * HEAD_W, (c + 1) * HEAD_W)
        kn = _norm_block128(k[:, sl], gck_ref[:, sl])
        mk_ref[:, sl] = kn
        mkb_ref[:, sl] = kn.astype(BF16)
    v = jnp.dot(h, w_ref[:, SEG:2 * SEG], preferred_element_type=F32)
    mv_ref[...] = v
    mvb_ref[...] = v.astype(BF16)


def _memkv(mem, mg, w_kv, gck):
    b = mem.shape[0]
    blk = lambda width: pl.BlockSpec((None, N_MEM, width), lambda bi: (bi, 0, 0))
    const = lambda shape: pl.BlockSpec(shape, lambda bi: (0,) * len(shape))
    return pl.pallas_call(
        _memkv_kernel,
        out_shape=(jax.ShapeDtypeStruct((b, N_MEM, SEG), F32), jax.ShapeDtypeStruct((b, N_MEM, SEG), F32),
                   jax.ShapeDtypeStruct((b, N_MEM, SEG), BF16), jax.ShapeDtypeStruct((b, N_MEM, SEG), BF16)),
        grid=(b,),
        in_specs=[blk(D_MODEL), const((1, D_MODEL)), const((D_MODEL, 2 * SEG)), const((1, SEG))],
        out_specs=(blk(SEG), blk(SEG), blk(SEG), blk(SEG)),
        compiler_params=pltpu.CompilerParams(dimension_semantics=("arbitrary",),
                                             vmem_limit_bytes=VMEM_LIMIT),
        name="mem_kv",
    )(mem, mg, w_kv, gck)


def _attn_kernel(pt_ref, rel_ref, x_ref, qb_ref, kb_ref, vb_ref, cqb_ref, sgb_ref, sgc_ref, mixa_ref,
                 mkb_ref, mvb_ref, wout_ref, hg_ref, lq_ref, lk_ref,
                 sin_hbm, memk_hbm, memv_hbm, kt_hbm, v_hbm,
                 o_ref, sout_hbm,
                 bdiag_ref, bsub_ref, qq_ref, m_ref, l_ref, acc_ref, mix_ref,
                 kbuf, vbuf, ring_sem, sin_buf, mk_buf, mv_buf, fetch_sem, stage, out_sem,
                 qbd_ref, sm_ref, sl_ref, sacc_ref, blast_ref, bnew_ref, kpad_ref, vpad_ref,
                 *, n_seq, n_mem_seq, n_new):
    t = ATT_TILE
    npg = PAGES_PER_TILE
    r8 = SAMPLE_ROWS
    hr = 2 * r8
    groups_per_seq = pt_ref.shape[1] // npg
    assert groups_per_seq & (groups_per_seq - 1) == 0
    seq_shift = groups_per_seq.bit_length() - 1
    bi = pl.program_id(0)
    qi = pl.program_id(1)
    nqb = kb_ref.shape[0] // t
    first_step = (bi == 0) & (qi == 0)
    last_step = (bi == pl.num_programs(0) - 1) & (qi == pl.num_programs(1) - 1)
    u_base = bi * (nqb * (nqb + 1) // 2) + jnp.right_shift(qi * (qi + 1), 1)

    lane = lax.broadcasted_iota(jnp.int32, (1, SEG), 1)

    def static(v):
        return isinstance(v, int)

    def ring_slot(pos):
        return pos % RING if static(pos) else lax.rem(pos, jnp.int32(RING))

    def clamp(v, hi):
        return min(v, hi) if static(v) else jnp.minimum(v, hi)

    def ring_copies(group, slot):
        out = []
        for i in range(npg):
            if group is None:
                pid = 0
            else:
                pid = pt_ref[jnp.right_shift(group, seq_shift), (group & (groups_per_seq - 1)) * npg + i]
            out.append(pltpu.make_async_copy(kt_hbm.at[pid], kbuf.at[slot, i], ring_sem.at[slot]))
            out.append(pltpu.make_async_copy(v_hbm.at[pid], vbuf.at[slot, i], ring_sem.at[slot]))
        return out

    n_real_groups = n_mem_seq * groups_per_seq

    def ring_start(pos):
        def go():
            for n, cp in enumerate(ring_copies(pos, ring_slot(pos))):
                cp.start(priority=n % 2)
        if static(pos):
            assert pos < n_real_groups
            go()
        else:
            pl.when(pos < n_real_groups)(go)

    def ring_wait(pos):
        def go():
            for cp in ring_copies(None, ring_slot(pos)):
                cp.wait()
        pl.when(pos < n_real_groups)(go)

    def fetch_copies(seq):
        slot = seq & 1
        s_in = clamp(seq, n_seq - 1)
        s_mem = clamp(seq, n_mem_seq - 1)
        return (pltpu.make_async_copy(sin_hbm.at[s_in], sin_buf.at[slot], fetch_sem.at[slot]),
                pltpu.make_async_copy(memk_hbm.at[s_mem], mk_buf.at[slot], fetch_sem.at[slot]),
                pltpu.make_async_copy(memv_hbm.at[s_mem], mv_buf.at[slot], fetch_sem.at[slot]))

    def out_copy(row, slot):
        return pltpu.make_async_copy(stage.at[slot], sout_hbm.at[row], out_sem.at[slot])

    def sample_begin(slot):
        q8 = sin_buf[slot, 0:r8, :].astype(BF16)
        zero = jnp.zeros_like(q8)
        for hm in range(2 * HEADS):
            qbd_ref[hm * r8:(hm + 1) * r8, :] = jnp.where(
                (lane >= hm * MAP_W) & (lane < (hm + 1) * MAP_W), q8, zero)
        sm_ref[...] = jnp.full(sm_ref.shape, NEG_INF, F32)
        sl_ref[...] = jnp.zeros(sl_ref.shape, F32)
        sacc_ref[...] = jnp.zeros(sacc_ref.shape, F32)

    def sample_attend(s_parts, v_of):
        n = len(s_parts)
        s = jnp.concatenate(s_parts, axis=1) if n > 1 else s_parts[0]
        m_old = sm_ref[...]
        m_new = jnp.maximum(m_old, jnp.max(s, axis=-1, keepdims=True))
        alpha = jnp.exp(m_old - m_new)
        p = jnp.exp(s - (jnp.concatenate([m_new] * n, axis=1) if n > 1 else m_new))
        psum = p[:, 0:PAGE]
        for i in range(1, n):
            psum = psum + p[:, i * PAGE:(i + 1) * PAGE]
        sl_ref[...] = alpha * sl_ref[...] + psum
        pb = p.astype(BF16)
        heads = []
        for h in range(HEADS):
            pv = None
            for i in range(n):
                part = jnp.dot(pb[h * hr:(h + 1) * hr, i * PAGE:(i + 1) * PAGE], v_of(i, h),
                               preferred_element_type=F32)
                pv = part if pv is None else pv + part
            heads.append(pv)
        sacc_ref[...] = alpha * sacc_ref[...] + jnp.concatenate(heads, axis=0)
        sm_ref[...] = m_new

    def sample_scores(u):
        slot = ring_slot(u)
        is_last_group = (u & (groups_per_seq - 1)) == groups_per_seq - 1
        last_bias = jnp.where(is_last_group, blast_ref[...], 0.0)
        qbd = qbd_ref[...]
        s_parts = []
        for i in range(npg):
            s = jnp.dot(qbd, kbuf[slot, i].astype(BF16), preferred_element_type=F32)
            if i == npg - 1:
                s = s + last_bias
            s_parts.append(s)
        return s_parts

    def sample_values(u, s_parts):
        slots = [ring_slot(u + g) for g in range(len(s_parts) // npg)]
        sample_attend(s_parts, lambda i, h: vbuf[slots[i // npg], i % npg,
                                                 pl.ds(h, PAGE, stride=HEADS), :].astype(BF16))

    def sample_end(u):
        seq = jnp.right_shift(u, seq_shift)
        slot = seq & 1
        out_copy(0, slot).wait()

        cq8 = sin_buf[slot, r8:2 * r8, :].astype(BF16)
        cross_scores = []
        for h in range(HEADS):
            mk_h = mk_buf[slot, pl.ds(h, N_MEM, stride=HEADS), :].astype(BF16)
            cross_scores.append(lax.dot_general(cq8[:, h * HEAD_W:(h + 1) * HEAD_W], mk_h, _NT,
                                                preferred_element_type=F32) * CROSS_SCALE)
        kpad_ref[0:r8, :] = sin_buf[slot, 4 * r8:5 * r8, :].astype(BF16)
        vpad_ref[0:r8, :] = sin_buf[slot, 5 * r8:6 * r8, :].astype(BF16)
        s_new = lax.dot_general(qbd_ref[...], kpad_ref[...], _NT, preferred_element_type=F32) + bnew_ref[...]
        sample_attend([s_new], lambda i, h: vpad_ref[:, h * HEAD_W:(h + 1) * HEAD_W])

        cross_p = [jnp.exp(sc - jnp.max(sc, axis=-1, keepdims=True)) for sc in cross_scores]
        lam = _lam(lq_ref, lk_ref)
        o = sacc_ref[...] / jnp.sum(sl_ref[...], axis=-1, keepdims=True)
        for h in range(HEADS):
            sl = slice(h * HEAD_W, (h + 1) * HEAD_W)
            mv_h = mv_buf[slot, pl.ds(h, N_MEM, stride=HEADS), :].astype(BF16)
            oc = jnp.dot(cross_p[h].astype(BF16), mv_h, preferred_element_type=F32)
            oc = oc / jnp.sum(cross_p[h], axis=-1, keepdims=True)
            stage[slot, r8:2 * r8, sl] = oc * sin_buf[slot, 3 * r8:4 * r8, sl]
            a = o[h * hr:h * hr + r8, :] - lam * o[h * hr + r8:(h + 1) * hr, :]
            od = _norm_block128(a, hg_ref[:, sl]) * (1.0 - LAM_INIT)
            stage[slot, 0:r8, sl] = od * sin_buf[slot, 2 * r8:3 * r8, sl]
        out_copy(seq, slot).start()
        for cp in fetch_copies(seq + 1):
            cp.wait()
        sample_begin(1 - slot)
        for cp in fetch_copies(seq + 2):
            cp.start()

    @pl.when(first_step)
    def _():
        for g in range(LOOKAHEAD):
            ring_start(g)
        for cp in fetch_copies(0):
            cp.start()
        d = lax.broadcasted_iota(jnp.int32, (t, t), 0) - lax.broadcasted_iota(jnp.int32, (t, t), 1)
        for h in range(HEADS):
            bdiag_ref[h] = jnp.where(d >= 0, _shifted_bias(d, rel_ref, h) * LOG2E, NEG_INF)
            bsub_ref[h] = _shifted_bias(d + t, rel_ref, h) * LOG2E
        tok = lax.broadcasted_iota(jnp.int32, (hr, PAGE), 0) & (r8 - 1)
        key = lax.broadcasted_iota(jnp.int32, (hr, PAGE), 1)
        for h in range(HEADS):
            blast_ref[h * hr:(h + 1) * hr, :] = _shifted_bias(PAGE + tok - key, rel_ref, h)
            ok = (key <= tok) & (key < n_new)
            bnew_ref[h * hr:(h + 1) * hr, :] = jnp.where(ok, _shifted_bias(tok - key, rel_ref, h), NEG_INF)
        kpad_ref[...] = jnp.zeros(kpad_ref.shape, BF16)
        vpad_ref[...] = jnp.zeros(vpad_ref.shape, BF16)
        stage[...] = jnp.zeros(stage.shape, F32)
        out_copy(n_seq, 0).start()
        out_copy(n_seq + 1, 1).start()
        for cp in fetch_copies(0):
            cp.wait()
        sample_begin(0)
        for cp in fetch_copies(1):
            cp.start()

    lam = _lam(lq_ref, lk_ref)
    lo = lax.broadcasted_iota(jnp.int32, (1, HEAD_W), 1) < MAP_W
    mix_ref[:, 0:SEG] = mixa_ref[...]

    for h in range(HEADS):
        qh = qb_ref[:, h * HEAD_W:(h + 1) * HEAD_W]
        zero = jnp.zeros_like(qh)
        qq_ref[h, 0:t, :] = jnp.where(lo, qh, zero)
        qq_ref[h, t:2 * t, :] = jnp.where(lo, zero, qh)
    def tile_step(j0, bias_ref, u, n):
        w = n * t
        for g in range(n):
            ring_start(u + LOOKAHEAD + g)
        for g in range(n):
            ring_wait(u + g)
        s_parts = []
        for g in range(n):
            s_parts += sample_scores(u + g)
        def head_scores(h):
            s = lax.dot_general(qq_ref[h], kb_ref[pl.ds(j0, w), h * HEAD_W:(h + 1) * HEAD_W], _NT,
                                preferred_element_type=F32)
            if bias_ref is not None:
                bias = bias_ref[h]
                s = s + jnp.concatenate([bias, bias], axis=0)
            return s

        scores = [head_scores(h) for h in range(HEADS)] if n > 1 else None
        for h in range(HEADS):
            sl = slice(h * HEAD_W, (h + 1) * HEAD_W)
            s = scores[h] if n > 1 else head_scores(h)
            if bias_ref is bdiag_ref:
                m_new = jnp.broadcast_to(jnp.max(s, axis=-1, keepdims=True), (2 * t, HEAD_W))
                p = jnp.exp2(s - jnp.concatenate([m_new] * (w // HEAD_W), axis=1))
                psum = p[:, 0:HEAD_W]
                for c in range(1, w // HEAD_W):
                    psum = psum + p[:, c * HEAD_W:(c + 1) * HEAD_W]
                l_ref[h] = psum
                acc_ref[h] = jnp.dot(p.astype(BF16), vb_ref[pl.ds(j0, w), sl], preferred_element_type=F32)
                m_ref[h] = m_new
                continue
            m_old = m_ref[h]
            m_new = jnp.maximum(m_old, jnp.max(s, axis=-1, keepdims=True))
            alpha = jnp.exp2(m_old - m_new)
            p = jnp.exp2(s - jnp.concatenate([m_new] * (w // HEAD_W), axis=1))
            psum = p[:, 0:HEAD_W]
            for c in range(1, w // HEAD_W):
                psum = psum + p[:, c * HEAD_W:(c + 1) * HEAD_W]
            l_ref[h] = alpha * l_ref[h] + psum
            acc_ref[h] = alpha * acc_ref[h] + jnp.dot(p.astype(BF16), vb_ref[pl.ds(j0, w), sl],
                                                      preferred_element_type=F32)
            m_ref[h] = m_new
        sample_values(u, s_parts)

        u_last = u + n - 1

        @pl.when((u_last & (groups_per_seq - 1)) == groups_per_seq - 1)
        def _():
            sample_end(u_last)

    tile_step(pl.multiple_of(qi * t, t), bdiag_ref, u_base, 1)

    @pl.when(qi >= 1)
    def _():
        tile_step(pl.multiple_of((qi - 1) * t, t), bsub_ref, u_base + 1, 1)

    n_far = jnp.maximum(qi - 1, 0)
    u_far = u_base + 2
    lead = jnp.where((n_far >= 1) & ((u_far & 1) == 1), 1, 0)
    n_pairs = jnp.right_shift(n_far - lead, 1)
    trail = (n_far - lead) & 1

    @pl.when(lead == 1)
    def _():
        tile_step(0, None, u_far, 1)

    def far_pair(k, carry):
        j = lead + 2 * k
        tile_step(pl.multiple_of(j * t, t), None, u_far + j, 2)
        return carry

    lax.fori_loop(0, n_pairs, far_pair, 0)

    @pl.when(trail == 1)
    def _():
        tile_step(pl.multiple_of((n_far - 1) * t, t), None, u_far + n_far - 1, 1)

    cross_scores = [lax.dot_general(cqb_ref[:, h * HEAD_W:(h + 1) * HEAD_W], mkb_ref[:, h * HEAD_W:(h + 1) * HEAD_W],
                                    _NT, preferred_element_type=F32) * CROSS_SCALE for h in range(HEADS)]
    for h in range(HEADS):
        sl = slice(h * HEAD_W, (h + 1) * HEAD_W)
        o = acc_ref[h] / jnp.sum(l_ref[h], axis=-1, keepdims=True)
        a = o[0:t, :] - lam * o[t:2 * t, :]
        od = _norm_block128(a, hg_ref[:, sl]) * (1.0 - LAM_INIT)
        mix_ref[:, SEG + h * HEAD_W:SEG + (h + 1) * HEAD_W] = (od * sgb_ref[:, sl].astype(F32)).astype(BF16)

        sc = cross_scores[h]
        pc = jnp.exp(sc - jnp.max(sc, axis=-1, keepdims=True))
        c = jnp.dot(pc.astype(BF16), mvb_ref[:, sl], preferred_element_type=F32)
        c = c / jnp.sum(pc, axis=-1, keepdims=True)
        mix_ref[:, 2 * SEG + h * HEAD_W:2 * SEG + (h + 1) * HEAD_W] = (c * sgc_ref[:, sl].astype(F32)).astype(BF16)

    o_ref[...] = x_ref[...] + jnp.dot(mix_ref[...], wout_ref[...], preferred_element_type=F32)

    @pl.when(last_step)
    def _():
        for cp in fetch_copies(n_seq + 1):
            cp.wait()
        out_copy(0, 0).wait()
        out_copy(0, 1).wait()


def _attn(page_table, rel_flat, x, qb, kb, vb, cqb, sgb, sgc, mixa, mkb, mvb, w_out, hg, lq, lk,
          sin, mem_k3, mem_v3, cache_kt, cache_v3, n_new):
    b, s, _ = x.shape
    t = ATT_TILE
    nqb = s // t
    n_tile_steps = b * nqb * (nqb + 1) // 2
    n_seq = sin.shape[0]
    assert n_tile_steps * PAGES_PER_TILE == n_seq * page_table.shape[1]
    assert sin.shape == (n_seq, SIN_ROWS, SEG) and page_table.shape[0] == mem_k3.shape[0]
    row = lambda width: pl.BlockSpec((None, t, width), lambda bi, i, pt: (bi, i, 0))
    seq = lambda n: pl.BlockSpec((None, n, SEG), lambda bi, i, pt: (bi, 0, 0), pipeline_mode=pl.Buffered(1))
    const = lambda shape: pl.BlockSpec(shape, lambda bi, i, pt: (0,) * len(shape))
    hbm = pl.BlockSpec(memory_space=pl.ANY)
    r8 = SAMPLE_ROWS
    nq = HEADS * 2 * r8
    grid_spec = pltpu.PrefetchScalarGridSpec(
        num_scalar_prefetch=1,
        grid=(b, nqb),
        in_specs=[pl.BlockSpec(memory_space=pltpu.SMEM),
                  row(D_MODEL), row(SEG), seq(s), seq(s), row(SEG), row(SEG), row(SEG), row(SEG),
                  seq(N_MEM), seq(N_MEM), const((3 * SEG, D_MODEL)), const((1, SEG)),
                  const((2, MAP_W)), const((2, MAP_W)), hbm, hbm, hbm, hbm, hbm],
        out_specs=(row(D_MODEL), hbm),
        scratch_shapes=[
            pltpu.VMEM((HEADS, t, t), F32), pltpu.VMEM((HEADS, t, t), F32),
            pltpu.VMEM((HEADS, 2 * t, HEAD_W), BF16),
            pltpu.VMEM((HEADS, 2 * t, HEAD_W), F32), pltpu.VMEM((HEADS, 2 * t, HEAD_W), F32),
            pltpu.VMEM((HEADS, 2 * t, HEAD_W), F32), pltpu.VMEM((t, 3 * SEG), BF16),
            pltpu.VMEM((RING, PAGES_PER_TILE, SEG, PAGE), F32),
            pltpu.VMEM((RING, PAGES_PER_TILE, PAGE * HEADS, HEAD_W), F32),
            pltpu.SemaphoreType.DMA((RING,)),
            pltpu.VMEM((2, SIN_ROWS, SEG), F32), pltpu.VMEM((2, N_MEM * HEADS, HEAD_W), F32),
            pltpu.VMEM((2, N_MEM * HEADS, HEAD_W), F32), pltpu.SemaphoreType.DMA((2,)),
            pltpu.VMEM((2, 2 * r8, SEG), F32), pltpu.SemaphoreType.DMA((2,)),
            pltpu.VMEM((nq, SEG), BF16),
            pltpu.VMEM((nq, HEAD_W), F32), pltpu.VMEM((nq, HEAD_W), F32), pltpu.VMEM((nq, HEAD_W), F32),
            pltpu.VMEM((nq, PAGE), F32), pltpu.VMEM((nq, PAGE), F32),
            pltpu.VMEM((PAGE, SEG), BF16), pltpu.VMEM((PAGE, SEG), BF16)],
    )
    kern = functools.partial(_attn_kernel, n_seq=n_seq, n_mem_seq=mem_k3.shape[0], n_new=n_new)
    return pl.pallas_call(
        kern,
        out_shape=(jax.ShapeDtypeStruct((b, s, D_MODEL), F32),
                   jax.ShapeDtypeStruct((n_seq + 2, 2 * r8, SEG), F32)),
        grid_spec=grid_spec,
        compiler_params=pltpu.CompilerParams(dimension_semantics=("arbitrary", "arbitrary"),
                                             vmem_limit_bytes=VMEM_LIMIT_ATTN),
        name="attn",
    )(page_table, rel_flat, x, qb, kb, vb, cqb, sgb, sgc, mixa, mkb, mvb, w_out, hg, lq, lk,
      sin, mem_k3, mem_v3, cache_kt, cache_v3)


def _inproj_sample_kernel(x_ref, st_ref, ng_ref, w_ref, gq_ref, gk_ref, gcq_ref, pw_ref, ps_ref,
                          k_ref, v_ref, pool_ref, q_ref, cq_ref, sgb_ref, sgc_ref, mixa_ref, d_ref):
    nb = st_ref.shape[1]
    nt = x_ref.shape[0] // nb
    h = _rms(x_ref[...], ng_ref[...]).astype(BF16)
    u = jnp.dot(h, w_ref[:, 0:SEG], preferred_element_type=F32)
    sga = _silu(jnp.dot(h, w_ref[:, SEG:2 * SEG], preferred_element_type=F32))

    def ext(e, sl):
        if e < POOL_PAD:
            return st_ref[e, :, sl]
        return u[(e - POOL_PAD) * nb:(e - POOL_PAD + 1) * nb, sl]

    for g, win in enumerate(POOL_WINDOWS):
        sl = slice(g * HEAD_W, (g + 1) * HEAD_W)
        for tk in range(nt):
            acc = ext(POOL_PAD + tk, sl)
            for j in range(1, win):
                acc = acc + ext(POOL_PAD + tk - j, sl)
            d_ref[tk * nb:(tk + 1) * nb, sl] = acc / float(win) - ext(POOL_PAD + tk, sl)
        y = jnp.dot(d_ref[:, sl].astype(BF16), pw_ref[g], preferred_element_type=F32) * ps_ref[:, sl]
        mixa_ref[:, sl] = y * sga[:, sl]

    keep = POOL_PAD - nt
    pool_ref[0:keep] = st_ref[nt:POOL_PAD]
    for tk in range(nt):
        pool_ref[keep + tk] = u[tk * nb:(tk + 1) * nb, :]

    _project_segments(h, w_ref, gq_ref, gk_ref, gcq_ref, k_ref, v_ref, q_ref, None, None,
                      cq_ref, sgb_ref, sgc_ref, q_scale=DIFF_SCALE, v_head_rows=False)


def _inproj_sample(x_tm, state_tm, ng, w_in, gq, gk, gcq, pw, ps):
    rows = x_tm.shape[0]
    nb = state_tm.shape[1]
    vmem = pl.BlockSpec(memory_space=pltpu.VMEM)
    mat = jax.ShapeDtypeStruct((rows, SEG), F32)
    return pl.pallas_call(
        _inproj_sample_kernel,
        out_shape=(mat, mat, jax.ShapeDtypeStruct((POOL_PAD, nb, SEG), F32), mat, mat, mat, mat, mat),
        in_specs=[vmem] * 9,
        out_specs=(vmem,) * 8,
        scratch_shapes=[pltpu.VMEM((rows, SEG), F32)],
        compiler_params=pltpu.CompilerParams(vmem_limit_bytes=VMEM_LIMIT),
        name="inproj_sample",
    )(x_tm, state_tm, ng, w_in, gq, gk, gcq, pw, ps)


def _outproj_kernel(x_ref, a_ref, d_ref, c_ref, w_ref, o_ref):
    y = jnp.dot(a_ref[...].astype(BF16), w_ref[0:SEG, :], preferred_element_type=F32)
    y = y + jnp.dot(d_ref[...].astype(BF16), w_ref[SEG:2 * SEG, :], preferred_element_type=F32)
    y = y + jnp.dot(c_ref[...].astype(BF16), w_ref[2 * SEG:3 * SEG, :], preferred_element_type=F32)
    o_ref[...] = x_ref[...] + y


def _outproj(x, a, d, c, w_out):
    rows = x.shape[0]
    vmem = pl.BlockSpec(memory_space=pltpu.VMEM)
    return pl.pallas_call(
        _outproj_kernel,
        out_shape=jax.ShapeDtypeStruct((rows, D_MODEL), F32),
        in_specs=[vmem] * 5,
        out_specs=vmem,
        compiler_params=pltpu.CompilerParams(vmem_limit_bytes=VMEM_LIMIT),
        name="outproj_sample",
    )(x, a, d, c, w_out)


def _tile_gain(g, reps):
    return jnp.tile(g.reshape(1, -1), (1, reps))


def kernel(x_prompt, x_sample, mem_prompt, cache_k, cache_v, cache_mem_k, cache_mem_v, state_pool, page_table,
           norm_g, w_in, q_norm_g, k_norm_g, cq_norm_g, ck_norm_g, mem_norm_g, w_mem_kv, lam_q, lam_k,
           head_norm_g, pool_w, pool_scale, rel_bias, w_out):
    depth = w_in.shape[0]
    assert depth == 1
    l = 0
    b, s, _ = x_prompt.shape
    nb, nt, _ = x_sample.shape
    r8 = SAMPLE_ROWS

    ng = norm_g[l].reshape(1, D_MODEL)
    w_in_b = w_in[l].astype(BF16)
    gq = _tile_gain(q_norm_g[l], SEG // MAP_W)
    gk = _tile_gain(k_norm_g[l], SEG // MAP_W)
    gcq = _tile_gain(cq_norm_g[l], HEADS)
    gck = _tile_gain(ck_norm_g[l], HEADS)
    pw = pool_w[l].astype(BF16)
    ps = pool_scale[l].reshape(1, SEG)
    hg = head_norm_g[l].reshape(1, SEG)
    w_out_b = w_out[l].astype(BF16)
    rel_flat = rel_bias.reshape(-1)

    k_p, v_p, pool_p, qb, kb, vb, cqb, sgb, sgc, mixa = _inproj_prompt(x_prompt, ng, w_in_b, gq, gk, gcq, pw, ps)
    mk, mv, mkb, mvb = _memkv(mem_prompt, mem_norm_g[l].reshape(1, D_MODEL), w_mem_kv[l].astype(BF16), gck)
    x_tm = x_sample.transpose(1, 0, 2).reshape(nt * nb, D_MODEL)
    st_tm = state_pool[l].transpose(1, 0, 2)
    k_s, v_s, pool_s, q_s, cq_s, sgb_s, sgc_s, mixa_s = _inproj_sample(x_tm, st_tm, ng, w_in_b, gq, gk, gcq, pw, ps)

    def seq_major(a, rows):
        a = a.reshape(nt, nb, SEG).transpose(1, 0, 2)
        return a if rows == nt else jnp.pad(a, ((0, 0), (0, rows - nt), (0, 0)))

    kn = seq_major(k_s, nt)
    vn = seq_major(v_s, nt)

    nqb = s // ATT_TILE
    n_tile_steps = b * nqb * (nqb + 1) // 2
    n_pages = page_table.shape[1]
    n_seq = n_tile_steps * PAGES_PER_TILE // n_pages
    assert n_seq >= nb and n_seq * n_pages == n_tile_steps * PAGES_PER_TILE
    sin = jnp.concatenate([seq_major(q_s, r8), seq_major(cq_s, r8), seq_major(sgb_s, r8), seq_major(sgc_s, r8),
                           jnp.pad(kn, ((0, 0), (0, r8 - nt), (0, 0))), jnp.pad(vn, ((0, 0), (0, r8 - nt), (0, 0)))],
                          axis=1)
    sin = jnp.pad(sin, ((0, n_seq - nb), (0, 0), (0, 0)))
    n_pool = cache_k.shape[1]
    cache_kt = jnp.transpose(cache_k[l], (0, 2, 3, 4, 1)).reshape(n_pool, SEG, PAGE)
    cache_v3 = cache_v[l].reshape(n_pool, PAGE * HEADS, HEAD_W)
    mem_k3 = cache_mem_k[l].reshape(nb, N_MEM * HEADS, HEAD_W)
    mem_v3 = cache_mem_v[l].reshape(nb, N_MEM * HEADS, HEAD_W)
    y_p, sres = _attn(page_table, rel_flat, x_prompt, qb, kb, vb, cqb, sgb, sgc, mixa, mkb, mvb, w_out_b, hg,
                      lam_q[l], lam_k[l], sin, mem_k3, mem_v3, cache_kt, cache_v3, nt)
    mixd = sres[:nb, 0:r8]
    mixc = sres[:nb, r8:2 * r8]

    x8 = jnp.pad(x_sample, ((0, 0), (0, r8 - nt), (0, 0))).reshape(nb * r8, D_MODEL)
    y8 = _outproj(x8, seq_major(mixa_s, r8).reshape(nb * r8, SEG), mixd.reshape(nb * r8, SEG),
                  mixc.reshape(nb * r8, SEG), w_out_b)
    y_s = y8.reshape(nb, r8, D_MODEL)[:, :nt]

    return (y_p, y_s,
            k_p.reshape(1, b, s, HEADS, 2, MAP_W), v_p.reshape(1, b, s, HEADS, HEAD_W),
            pool_p[None], mk.reshape(1, b, N_MEM, HEADS, HEAD_W), mv.reshape(1, b, N_MEM, HEADS, HEAD_W),
            kn.reshape(1, nb, nt, HEADS, 2, MAP_W), vn.reshape(1, nb, nt, HEADS, HEAD_W),
            pool_s.transpose(1, 0, 2)[None])
```
